```python
import math
import jax, jax.numpy as jnp
from jax import lax
import numpy as np

D_MODEL = 2048
BATCH = 8
SEQ = 2048
DEPTH = 2

GRID_W = 64
EPS = 1e-6
WIDTH_A = D_MODEL // 2
WIDTH_B = D_MODEL // 2
HG_DK = 128
HG_DV = 128
HG_HEADS = WIDTH_A // HG_DV
HG_CHUNK = 32
NA_DH = 64
NA_HEADS = WIDTH_B // NA_DH
NA_KH_MAX = 8
NA_KW = 16
DA_DH = 128
DA_DV = 2 * DA_DH
DA_HEADS = D_MODEL // (2 * DA_DH)
DA_QK_WIDTH = 2 * DA_HEADS * DA_DH
ROPE_THETA = 500000.0
ROPE_DIM = DA_DH // 4
Q_BLOCK = 128
N_EXPERTS = 16
N_GROUPS = 4
EXPERTS_PER_GROUP = N_EXPERTS // N_GROUPS
TOP_K = 2
GROUP_SCORE_K = 2
D_FF = D_MODEL // 2
EVEN_IN_WIDTH = 5 * WIDTH_A + 3 * WIDTH_B
ODD_IN_WIDTH = 2 * DA_QK_WIDTH + DA_HEADS * DA_DV

kernel_name = 'hybrid_hgrn2_natten_diffattn_moe'


def rmsnorm(x, g):
    x32 = x.astype(jnp.float32)
    y = x32 * lax.rsqrt(jnp.mean(x32 * x32, axis=-1, keepdims=True) + EPS)
    return y.astype(x.dtype) * g


def modulate(h, shift, scale):
    return h * (1.0 + scale[:, None, :]) + shift[:, None, :]


def to_heads(x, n_heads):
    B, T, W = x.shape
    return x.reshape(B, T, n_heads, W // n_heads).transpose(0, 2, 1, 3)


def gla_chunked(q, k, v, g):
    B, H, T, dk = q.shape
    dv = v.shape[-1]
    n = T // HG_CHUNK
    q, k, g = [a.astype(jnp.float32).reshape(B, H, n, HG_CHUNK, dk) for a in (q, k, g)]
    v = v.astype(jnp.float32).reshape(B, H, n, HG_CHUNK, dv)
    G = jnp.cumsum(g, axis=3)
    G_last = G[:, :, :, -1:, :]
    qg = q * jnp.exp(G)
    kg = k * jnp.exp(-G)
    lower_tri = jnp.tril(jnp.ones((HG_CHUNK, HG_CHUNK), dtype=bool))
    A = jnp.where(lower_tri, jnp.einsum('bhncd,bhnsd->bhncs', qg, kg), 0.0)
    o_intra = jnp.einsum('bhncs,bhnsv->bhncv', A, v)
    U = jnp.einsum('bhncd,bhncv->bhndv', k * jnp.exp(G_last - G), v)
    decay = jnp.exp(G_last[:, :, :, 0, :])

    def step(S, inp):
        dec, u = inp
        return dec[..., None] * S + u, S

    S0 = jnp.zeros((B, H, dk, dv), jnp.float32)
    _, S_prev = lax.scan(step, S0, (jnp.moveaxis(decay, 2, 0), jnp.moveaxis(U, 2, 0)))
    S_prev = jnp.moveaxis(S_prev, 0, 2)
    o_inter = jnp.einsum('bhncd,bhndv->bhncv', qg, S_prev)
    return (o_intra + o_inter).reshape(B, H, T, dv)


def hgrn2_mixer(q, i, f_fwd, f_bwd, gate, lb, norm_g):
    dt = q.dtype
    qh = jax.nn.silu(to_heads(q, HG_HEADS))
    ih = to_heads(i, HG_HEADS)
    lb_h = lb.reshape(HG_HEADS, 1, HG_DK)

    def run_direction(f_logits, qd, vd):
        f = lb_h + (1.0 - lb_h) * jax.nn.sigmoid(to_heads(f_logits, HG_HEADS).astype(jnp.float32))
        return gla_chunked(qd, 1.0 - f, vd, jnp.log(f))

    o_fwd = run_direction(f_fwd, qh, ih)
    o_bwd = jnp.flip(run_direction(jnp.flip(f_bwd, 1), jnp.flip(qh, 2), jnp.flip(ih, 2)), 2)
    o = (o_fwd + o_bwd).astype(dt).transpose(0, 2, 1, 3)
    o = rmsnorm(o, norm_g)
    B, T = o.shape[:2]
    return o.reshape(B, T, HG_HEADS * HG_DV) * jax.nn.silu(gate)


def neighbourhood_attention(q, k, v, rpb):
    B, T, _ = q.shape
    rows = T // GRID_W
    kh = min(NA_KH_MAX, rows)
    q = q.reshape(B, rows, GRID_W, NA_HEADS, NA_DH)
    k = k.reshape(B, rows, GRID_W, NA_HEADS, NA_DH)
    v = v.reshape(B, rows, GRID_W, NA_HEADS, NA_DH)
    r = jnp.arange(rows)
    row_idx = jnp.clip(r - kh // 2, 0, rows - kh)[:, None] + jnp.arange(kh)[None, :]
    kg = k[:, row_idx]
    vg = v[:, row_idx]
    s = jnp.einsum('brchd,brkshd->bhrcks', q, kg).astype(jnp.float32) * (NA_DH ** -0.5)
    col = jnp.arange(GRID_W)
    col_start = jnp.clip(col - NA_KW // 2, 0, GRID_W - NA_KW)
    in_win = (col[None, :] >= col_start[:, None]) & (col[None, :] < col_start[:, None] + NA_KW)
    dr = row_idx - r[:, None] + (NA_KH_MAX - 1)
    dc = jnp.clip(col[None, :] - col[:, None] + (NA_KW - 1), 0, 2 * NA_KW - 2)
    bias = rpb[:, dr[:, None, :, None], dc[None, :, None, :]]
    s = jnp.where(in_win[:, None, :], s + bias[None].astype(jnp.float32), -1e30)
    shp = s.shape
    p = jax.nn.softmax(s.reshape(shp[:4] + (kh * GRID_W,)), axis=-1).reshape(shp).astype(v.dtype)
    o = jnp.einsum('bhrcks,brkshd->brchd', p, vg)
    return o.reshape(B, T, NA_HEADS * NA_DH)


def rope_partial(x, pos):
    half = ROPE_DIM // 2
    inv = ROPE_THETA ** (-jnp.arange(half, dtype=jnp.float32) / half)
    ang = pos[:, None] * inv[None, :]
    cos = jnp.cos(ang)[None, :, None, :].astype(x.dtype)
    sin = jnp.sin(ang)[None, :, None, :].astype(x.dtype)
    x1 = x[..., :half]
    x2 = x[..., half:ROPE_DIM]
    return jnp.concatenate([x1 * cos - x2 * sin, x1 * sin + x2 * cos, x[..., ROPE_DIM:]], axis=-1)


def even_mixer(h, w_in, hgrn_norm, na_rpb, w_out, lb):
    proj = h @ w_in
    q_hg, i_hg, f_fwd, f_bwd, g_hg, q_na, k_na, v_na = jnp.split(
        proj, np.cumsum([WIDTH_A] * 5 + [WIDTH_B] * 2).tolist(), axis=-1)
    o_a = hgrn2_mixer(q_hg, i_hg, f_fwd, f_bwd, g_hg, lb, hgrn_norm)
    o_b = neighbourhood_attention(q_na, k_na, v_na, na_rpb)
    return jnp.concatenate([o_a, o_b], axis=-1) @ w_out


def odd_mixer(h, w_in, diff_lambda, diff_norm, w_out, layer_idx):
    B, T, _ = h.shape
    proj = h @ w_in
    q, k, v = jnp.split(proj, [DA_QK_WIDTH, 2 * DA_QK_WIDTH], axis=-1)
    pos = jnp.arange(T, dtype=jnp.float32)
    q = rope_partial(q.reshape(B, T, 2 * DA_HEADS, DA_DH), pos).reshape(B, T, DA_HEADS, 2, DA_DH)
    k = rope_partial(k.reshape(B, T, 2 * DA_HEADS, DA_DH), pos).reshape(B, T, DA_HEADS, 2, DA_DH)
    v = v.reshape(B, T, DA_HEADS, DA_DV)
    lam_init = 0.8 - 0.6 * math.exp(-0.3 * layer_idx)
    lp = diff_lambda.astype(jnp.float32)
    lam = jnp.exp(jnp.sum(lp[0] * lp[1])) - jnp.exp(jnp.sum(lp[2] * lp[3])) + lam_init
    nb = T // Q_BLOCK
    qb = jnp.moveaxis(q.reshape(B, nb, Q_BLOCK, DA_HEADS, 2, DA_DH), 1, 0)

    def block(qblk):
        s = jnp.einsum('bqhid,bkhid->bhiqk', qblk, k).astype(jnp.float32) * (DA_DH ** -0.5)
        p = jax.nn.softmax(s, axis=-1)
        a = (p[:, :, 0] - lam * p[:, :, 1]).astype(v.dtype)
        return jnp.einsum('bhqk,bkhv->bqhv', a, v)

    o = lax.map(block, qb)
    o = jnp.moveaxis(o, 0, 1).reshape(B, T, DA_HEADS, DA_DV)
    o = rmsnorm(o, diff_norm) * (1.0 - lam_init)
    return o.reshape(B, T, DA_HEADS * DA_DV) @ w_out


def moe(h, router_w, router_b, wg, wu, wd):
    scores = jax.nn.sigmoid((h @ router_w).astype(jnp.float32))
    sel = scores + router_b.astype(jnp.float32)
    grp = sel.reshape(sel.shape[:-1] + (N_GROUPS, EXPERTS_PER_GROUP))
    grp_score = jnp.sum(lax.top_k(grp, GROUP_SCORE_K)[0], axis=-1)
    best_group = jnp.argmax(grp_score, axis=-1)
    expert_group = jnp.arange(N_EXPERTS) // EXPERTS_PER_GROUP
    masked = jnp.where(best_group[..., None] == expert_group, sel, -jnp.inf)
    _, idx = lax.top_k(masked, TOP_K)
    w = jnp.take_along_axis(scores, idx, axis=-1)
    w = w / jnp.sum(w, axis=-1, keepdims=True)
    gates = jnp.sum(jax.nn.one_hot(idx, N_EXPERTS, dtype=jnp.float32) * w[..., None], axis=-2)

    def per_sequence(args):
        xb, gb = args
        a = jnp.einsum('td,edf->tef', xb, wg)
        u = jnp.einsum('td,edf->tef', xb, wu)
        hid = jax.nn.silu(a) * u * gb[..., None].astype(xb.dtype)
        return jnp.einsum('tef,efd->td', hid, wd)

    return lax.map(per_sequence, (h, gates))


def setup_inputs(seed: int = 0) -> dict:
    key = jax.random.key(seed)
    ks = jax.random.split(key, 32)
    D = D_MODEL

    def nrm(i, shape, s):
        return jax.random.normal(ks[i], shape, jnp.float32) * s

    return {
        'x': nrm(0, (BATCH, SEQ, D), 1.0),
        'c': nrm(1, (BATCH, D), 1.0),
        'ln_attn_0': 1.0 + nrm(2, (D,), 0.05),
        'ada_w_0': nrm(3, (D, 6 * D), 0.5 * D ** -0.5),
        'ada_b_0': nrm(4, (6 * D,), 0.02),
        'w_in_0': nrm(5, (D, EVEN_IN_WIDTH), D ** -0.5),
        'hgrn_norm_0': 1.0 + nrm(6, (HG_HEADS, HG_DV), 0.05),
        'na_rpb_0': nrm(7, (NA_HEADS, 2 * NA_KH_MAX - 1, 2 * NA_KW - 1), 0.1),
        'w_out_0': nrm(8, (WIDTH_A + WIDTH_B, D), (WIDTH_A + WIDTH_B) ** -0.5),
        'ln_ffn_0': 1.0 + nrm(9, (D,), 0.05),
        'moe_wg_0': nrm(10, (N_EXPERTS, D, D_FF), D ** -0.5),
        'moe_wu_0': nrm(11, (N_EXPERTS, D, D_FF), D ** -0.5),
        'moe_wd_0': nrm(12, (N_EXPERTS, D_FF, D), D_FF ** -0.5),
        'ln_attn_1': 1.0 + nrm(13, (D,), 0.05),
        'ada_w_1': nrm(14, (D, 6 * D), 0.5 * D ** -0.5),
        'ada_b_1': nrm(15, (6 * D,), 0.02),
        'w_in_1': nrm(16, (D, ODD_IN_WIDTH), D ** -0.5),
        'diff_lambda_1': nrm(17, (4, DA_DH), 0.1),
        'diff_norm_1': 1.0 + nrm(18, (DA_DV,), 0.05),
        'w_out_1': nrm(19, (DA_HEADS * DA_DV, D), (DA_HEADS * DA_DV) ** -0.5),
        'ln_ffn_1': 1.0 + nrm(20, (D,), 0.05),
        'moe_wg_1': nrm(21, (N_EXPERTS, D, D_FF), D ** -0.5),
        'moe_wu_1': nrm(22, (N_EXPERTS, D, D_FF), D ** -0.5),
        'moe_wd_1': nrm(23, (N_EXPERTS, D_FF, D), D_FF ** -0.5),
        'hgrn_lb_logits': nrm(24, (DEPTH + 1, WIDTH_A), 0.1),
        'router_w': nrm(25, (D, N_EXPERTS), D ** -0.5),
        'router_b': nrm(26, (N_EXPERTS,), 0.01),
        'ln_final': 1.0 + nrm(27, (D,), 0.05),
    }


def reference(x, c, ln_attn_0, ada_w_0, ada_b_0, w_in_0, hgrn_norm_0, na_rpb_0, w_out_0,
              ln_ffn_0, moe_wg_0, moe_wu_0, moe_wd_0, ln_attn_1, ada_w_1, ada_b_1, w_in_1,
              diff_lambda_1, diff_norm_1, w_out_1, ln_ffn_1, moe_wg_1, moe_wu_1, moe_wd_1,
              hgrn_lb_logits, router_w, router_b, ln_final):
    lower_bounds = jnp.cumsum(jax.nn.softmax(hgrn_lb_logits.astype(jnp.float32), axis=0), axis=0)
    layers = [
        dict(ln_attn=ln_attn_0, ada_w=ada_w_0, ada_b=ada_b_0, ln_ffn=ln_ffn_0,
             mix=(w_in_0, hgrn_norm_0, na_rpb_0, w_out_0), moe=(moe_wg_0, moe_wu_0, moe_wd_0)),
        dict(ln_attn=ln_attn_1, ada_w=ada_w_1, ada_b=ada_b_1, ln_ffn=ln_ffn_1,
             mix=(w_in_1, diff_lambda_1, diff_norm_1, w_out_1), moe=(moe_wg_1, moe_wu_1, moe_wd_1)),
    ]
    c_act = jax.nn.silu(c)
    for l in range(DEPTH):
        p = layers[l]
        mod = c_act @ p['ada_w'] + p['ada_b']
        sh_a, sc_a, gt_a, sh_f, sc_f, gt_f = jnp.split(mod, 6, axis=-1)
        h = modulate(rmsnorm(x, p['ln_attn']), sh_a, sc_a)
        if l % 2 == 0:
            m = even_mixer(h, *p['mix'], lb=lower_bounds[l])
        else:
            m = odd_mixer(h, *p['mix'], layer_idx=l)
        x = x + gt_a[:, None, :] * m
        h = modulate(rmsnorm(x, p['ln_ffn']), sh_f, sc_f)
        x = x + gt_f[:, None, :] * moe(h, router_w, router_b, *p['moe'])
    return rmsnorm(x, ln_final)
```

```python
import functools
import math

import jax
import jax.numpy as jnp
from jax import lax
from jax.experimental import pallas as pl
from jax.experimental.pallas import tpu as pltpu

F32 = jnp.float32
BF16 = jnp.bfloat16
I32 = jnp.int32

EPS = 1e-6
LANES = 128
GRID_W = 64
HG_HEADS = 8
HG_D = 128
HG_CHUNK = 32
NA_HEADS = 16
NA_DH = 64
NA_KH = 8
NA_KW = 16
DA_HEADS = 8
DA_DH = 128
ROPE_THETA = 500000.0
ROPE_DIM = DA_DH // 4
N_EXPERTS = 16
N_GROUPS = 4
EXPERTS_PER_GROUP = 4
NEG_BIG = -1e30

VMEM_LIMIT = 56 * 1024 * 1024


def _cparams(sem):
    return pltpu.CompilerParams(dimension_semantics=sem, vmem_limit_bytes=VMEM_LIMIT)


def _dot(a, b):
    return jnp.dot(a, b, preferred_element_type=F32)


def _dot_nt(a, b):
    return lax.dot_general(a, b, (((1,), (1,)), ((), ())), preferred_element_type=F32)


def _silu(x):
    return x * jax.nn.sigmoid(x)


def _rms(x):
    return x * lax.rsqrt(jnp.mean(x * x, axis=-1, keepdims=True) + EPS)


def _ada_body(c_ref, w_ref, b_ref, o_ref):
    a = _silu(c_ref[...]).astype(BF16)
    o_ref[...] = _dot(a, w_ref[...].astype(BF16)) + b_ref[...]


def _ada(c, w, b):
    bsz, d = c.shape
    n = w.shape[1]
    rows = 16
    c16 = jnp.concatenate([c, jnp.zeros((rows - bsz, d), c.dtype)], axis=0)
    tn = 1024
    out = pl.pallas_call(
        _ada_body,
        grid=(n // tn,),
        in_specs=[pl.BlockSpec((rows, d), lambda j: (0, 0)),
                  pl.BlockSpec((d, tn), lambda j: (0, j)),
                  pl.BlockSpec((1, tn), lambda j: (0, j))],
        out_specs=pl.BlockSpec((rows, tn), lambda j: (0, j)),
        out_shape=jax.ShapeDtypeStruct((rows, n), F32),
        compiler_params=_cparams(("arbitrary",)),
        name="ada_mod",
    )(c16, w, b.reshape(1, n))
    return out[:bsz].reshape(bsz, 6, d)


def _norm_mod(x, g, sh, sc):
    return (_rms(x) * g) * (1.0 + sc) + sh


def _norm_body(x_ref, g_ref, mod_ref, o_ref, *, sh_idx, sc_idx):
    h = _norm_mod(x_ref[...], g_ref[...], mod_ref[0, sh_idx:sh_idx + 1, :], mod_ref[0, sc_idx:sc_idx + 1, :])
    o_ref[...] = h.astype(o_ref.dtype)


def _norm(x, g, mod, sh_idx, sc_idx, seq, tm=512):
    n, d = x.shape
    tpb = seq // tm
    return pl.pallas_call(
        functools.partial(_norm_body, sh_idx=sh_idx, sc_idx=sc_idx),
        grid=(n // tm,),
        in_specs=[pl.BlockSpec((tm, d), lambda i: (i, 0)),
                  pl.BlockSpec((1, d), lambda i: (0, 0)),
                  pl.BlockSpec((1, 6, d), lambda i: (i // tpb, 0, 0))],
        out_specs=pl.BlockSpec((tm, d), lambda i: (i, 0)),
        out_shape=jax.ShapeDtypeStruct((n, d), BF16),
        compiler_params=_cparams(("arbitrary",)),
        name="norm_mod",
    )(x, g.reshape(1, d), mod)


def _mm_body(a_ref, w_ref, o_ref):
    o_ref[...] = _dot(a_ref[...], w_ref[...]).astype(o_ref.dtype)


def _mm(a, w, tm=512, tn=1024):
    n, k = a.shape
    nout = w.shape[1]
    return pl.pallas_call(
        _mm_body,
        grid=(n // tm, nout // tn),
        in_specs=[pl.BlockSpec((tm, k), lambda i, j: (i, 0)),
                  pl.BlockSpec((k, tn), lambda i, j: (0, j))],
        out_specs=pl.BlockSpec((tm, tn), lambda i, j: (i, j)),
        out_shape=jax.ShapeDtypeStruct((n, nout), BF16),
        compiler_params=_cparams(("arbitrary", "arbitrary")),
        name="proj_in",
    )(a, w)


def _mm_rope_body(a_ref, w_ref, cos_ref, sa_ref, sb_ref, o_ref, *, n_rope_tiles, tn):
    acc = _dot(a_ref[...], w_ref[...])
    j = pl.program_id(1)

    @pl.when(j < n_rope_tiles)
    def _():
        cos, sa, sb = cos_ref[...], sa_ref[...], sb_ref[...]
        half = ROPE_DIM // 2
        for c in range(tn // LANES):
            seg = acc[:, c * LANES:(c + 1) * LANES]
            r = seg * cos + pltpu.roll(seg, LANES - half, 1) * sa + pltpu.roll(seg, half, 1) * sb
            o_ref[:, c * LANES:(c + 1) * LANES] = r.astype(o_ref.dtype)

    @pl.when(j >= n_rope_tiles)
    def _():
        o_ref[...] = acc.astype(o_ref.dtype)


def _rope_tables(seq):
    half = ROPE_DIM // 2
    inv = ROPE_THETA ** (-jnp.arange(half, dtype=F32) / half)
    ang = jnp.arange(seq, dtype=F32)[:, None] * inv[None, :]
    cos, sin = jnp.cos(ang), jnp.sin(ang)
    pad = jnp.zeros((seq, LANES - ROPE_DIM), F32)
    z = jnp.zeros((seq, half), F32)
    cos_t = jnp.concatenate([cos, cos, pad + 1.0], axis=1)
    sa_t = jnp.concatenate([-sin, z, pad], axis=1)
    sb_t = jnp.concatenate([z, sin, pad], axis=1)
    return cos_t, sa_t, sb_t


def _mm_rope(a, w, seq, rope_cols, tm=512, tn=1024):
    n, k = a.shape
    nout = w.shape[1]
    tpb = seq // tm
    cos_t, sa_t, sb_t = _rope_tables(seq)
    tab_spec = pl.BlockSpec((tm, LANES), lambda i, j: (i % tpb, 0))
    return pl.pallas_call(
        functools.partial(_mm_rope_body, n_rope_tiles=rope_cols // tn, tn=tn),
        grid=(n // tm, nout // tn),
        in_specs=[pl.BlockSpec((tm, k), lambda i, j: (i, 0)),
                  pl.BlockSpec((k, tn), lambda i, j: (0, j)),
                  tab_spec, tab_spec, tab_spec],
        out_specs=pl.BlockSpec((tm, tn), lambda i, j: (i, j)),
        out_shape=jax.ShapeDtypeStruct((n, nout), BF16),
        compiler_params=_cparams(("arbitrary", "arbitrary")),
        name="proj_in_rope",
    )(a, w, cos_t, sa_t, sb_t)


def _mm_res_body(*refs, n_a, gt_idx):
    a_refs = refs[:n_a]
    w_ref, x_ref, mod_ref, o_ref = refs[n_a:]
    acc = None
    k0 = 0
    for a_ref in a_refs:
        kk = a_ref.shape[1]
        part = _dot(a_ref[...], w_ref[k0:k0 + kk, :])
        acc = part if acc is None else acc + part
        k0 += kk
    o_ref[...] = x_ref[...] + mod_ref[0, gt_idx:gt_idx + 1, :] * acc


def _mm_res(a_list, w, x, mod, gt_idx, seq, tm=512, tn=1024):
    n, d = x.shape
    tpb = seq // tm
    in_specs = [pl.BlockSpec((tm, a.shape[1]), lambda i, j: (i, 0)) for a in a_list]
    in_specs += [pl.BlockSpec((w.shape[0], tn), lambda i, j: (0, j)),
                 pl.BlockSpec((tm, tn), lambda i, j: (i, j)),
                 pl.BlockSpec((1, 6, tn), lambda i, j: (i // tpb, 0, j))]
    return pl.pallas_call(
        functools.partial(_mm_res_body, n_a=len(a_list), gt_idx=gt_idx),
        grid=(n // tm, d // tn),
        in_specs=in_specs,
        out_specs=pl.BlockSpec((tm, tn), lambda i, j: (i, j)),
        out_shape=jax.ShapeDtypeStruct((n, d), F32),
        compiler_params=_cparams(("arbitrary", "arbitrary")),
        name="proj_out_res",
    )(*a_list, w, x, mod)


def _hgrn_body(q_ref, i_ref, ff_ref, fb_ref, gg_ref, lbl_ref, ng_ref, o_ref,
               qgf_s, qgb_s, utf_s, utb_s, decf_s, decb_s, oacc_s, *, seq):
    c_sz = HG_CHUNK
    n_chunks = seq // c_sz
    lg = lbl_ref[...]
    ex = jnp.exp(lg - jnp.max(lg, axis=0, keepdims=True))
    lb = ex[0:1, :] / jnp.sum(ex, axis=0, keepdims=True)

    row = lax.broadcasted_iota(I32, (c_sz, c_sz), 0)
    col = lax.broadcasted_iota(I32, (c_sz, c_sz), 1)
    lower = row >= col
    upper = row <= col
    ones_lo = lower.astype(BF16)
    ones_up = upper.astype(BF16)

    def split_cumsum(op, g):
        g_hi = g.astype(BF16)
        g_lo = (g - g_hi.astype(F32)).astype(BF16)
        return _dot(op, g_hi) + _dot(op, g_lo)

    def one_direction(fl, qh, v, op, mask, last_row):
        f = lb + (1.0 - lb) * jax.nn.sigmoid(fl)
        k = 1.0 - f
        g = jnp.log(f)
        gc = split_cumsum(op, g)
        g_end = gc[last_row:last_row + 1, :]
        qg = (qh * jnp.exp(gc)).astype(BF16)
        kg = (k * jnp.exp(-gc)).astype(BF16)
        kd = (k * jnp.exp(g_end - gc)).astype(BF16)
        a = jnp.where(mask, _dot_nt(qg, kg), 0.0).astype(BF16)
        o_intra = _dot(a, v)
        ut = lax.dot_general(v, kd, (((0,), (0,)), ((), ())), preferred_element_type=F32)
        return qg, o_intra, ut, jnp.exp(g_end)

    def stage1(c, carry):
        sl = pl.ds(pl.multiple_of(c * c_sz, c_sz), c_sz)
        qh = _silu(q_ref[0, sl, :].astype(F32))
        v = i_ref[0, sl, :]
        qgf, oif, utf, decf = one_direction(ff_ref[0, sl, :].astype(F32), qh, v, ones_lo, lower, c_sz - 1)
        qgb, oib, utb, decb = one_direction(fb_ref[0, sl, :].astype(F32), qh, v, ones_up, upper, 0)
        qgf_s[sl, :] = qgf
        qgb_s[sl, :] = qgb
        utf_s[c] = utf
        utb_s[c] = utb
        decf_s[c] = decf
        decb_s[c] = decb
        oacc_s[sl, :] = oif + oib
        return carry

    lax.fori_loop(0, n_chunks, stage1, 0)

    def stage2(c, carry):
        stf, stb = carry
        slf = pl.ds(pl.multiple_of(c * c_sz, c_sz), c_sz)
        cb = n_chunks - 1 - c
        slb = pl.ds(pl.multiple_of(cb * c_sz, c_sz), c_sz)
        oacc_s[slf, :] += _dot_nt(qgf_s[slf, :], stf.astype(BF16))
        oacc_s[slb, :] += _dot_nt(qgb_s[slb, :], stb.astype(BF16))
        stf = stf * decf_s[c] + utf_s[c]
        stb = stb * decb_s[cb] + utb_s[cb]
        return stf, stb

    zero = jnp.zeros((HG_D, HG_D), F32)
    lax.fori_loop(0, n_chunks, stage2, (zero, zero))

    y = _rms(oacc_s[...]) * ng_ref[0]
    o_ref[0] = (y * _silu(gg_ref[0].astype(F32))).astype(o_ref.dtype)


def _hgrn(proj, lb_logits, norm_g):
    bsz, seq, _ = proj.shape
    n_chunks = seq // HG_CHUNK
    n_layers = lb_logits.shape[0]

    def seg_spec(seg):
        return pl.BlockSpec((1, seq, HG_D), lambda b, h: (b, 0, seg * HG_HEADS + h))

    return pl.pallas_call(
        functools.partial(_hgrn_body, seq=seq),
        grid=(bsz, HG_HEADS),
        in_specs=[seg_spec(0), seg_spec(1), seg_spec(2), seg_spec(3), seg_spec(4),
                  pl.BlockSpec((n_layers, HG_D), lambda b, h: (0, h)),
                  pl.BlockSpec((1, 1, HG_D), lambda b, h: (h, 0, 0))],
        out_specs=pl.BlockSpec((1, seq, HG_D), lambda b, h: (b, 0, h)),
        out_shape=jax.ShapeDtypeStruct((bsz, seq, HG_HEADS * HG_D), BF16),
        scratch_shapes=[pltpu.VMEM((seq, HG_D), BF16), pltpu.VMEM((seq, HG_D), BF16),
                        pltpu.VMEM((n_chunks, HG_D, HG_D), F32), pltpu.VMEM((n_chunks, HG_D, HG_D), F32),
                        pltpu.VMEM((n_chunks, 1, HG_D), F32), pltpu.VMEM((n_chunks, 1, HG_D), F32),
                        pltpu.VMEM((seq, HG_D), F32)],
        compiler_params=_cparams(("arbitrary", "arbitrary")),
        name="hgrn2",
    )(proj, proj, proj, proj, proj, lb_logits, norm_g.reshape(HG_HEADS, 1, HG_D))


def _na_bias_table(rpb):
    col = jnp.arange(GRID_W)
    dc = jnp.clip(col[None, :] - col[:, None] + (NA_KW - 1), 0, 2 * NA_KW - 2)
    dr = jnp.arange(NA_KH)[None, :] - jnp.arange(NA_KH)[:, None] + (NA_KH - 1)
    t = rpb[:, dr[:, None, :, None], dc[None, :, None, :]]
    return t.reshape(rpb.shape[0], NA_KH, GRID_W, NA_KH * GRID_W)


def _na_body(q_ref, k_ref, v_ref, bt_ref, o_ref, *, seq):
    rows = seq // GRID_W
    win = NA_KH * GRID_W
    cq = lax.broadcasted_iota(I32, (GRID_W, win), 0)
    ck = lax.broadcasted_iota(I32, (GRID_W, win), 1) % GRID_W
    cs = jnp.clip(cq - NA_KW // 2, 0, GRID_W - NA_KW)
    in_win = (ck >= cs) & (ck < cs + NA_KW)
    scale = NA_DH ** -0.5

    def one_row(r, carry):
        start = jnp.clip(r - NA_KH // 2, 0, rows - NA_KH)
        pat = r - start
        qs = pl.ds(pl.multiple_of(r * GRID_W, GRID_W), GRID_W)
        ks = pl.ds(pl.multiple_of(start * GRID_W, GRID_W), win)
        for hh in range(LANES // NA_DH):
            ls = slice(hh * NA_DH, (hh + 1) * NA_DH)
            s = _dot_nt(q_ref[0, qs, ls], k_ref[0, ks, ls]) * scale
            s = jnp.where(in_win, s + bt_ref[hh, pat], NEG_BIG)
            e = jnp.exp(s - jnp.max(s, axis=-1, keepdims=True))
            o = _dot(e.astype(BF16), v_ref[0, ks, ls]) / jnp.sum(e, axis=-1, keepdims=True)
            o_ref[0, qs, ls] = o.astype(o_ref.dtype)
        return carry

    lax.fori_loop(0, rows, one_row, 0)


def _na(proj, rpb, col0):
    bsz, seq, _ = proj.shape
    hpb = LANES // NA_DH
    nblk = NA_HEADS // hpb
    b0 = col0 // LANES
    bt = _na_bias_table(rpb)

    def seg_spec(seg):
        return pl.BlockSpec((1, seq, LANES), lambda b, h: (b, 0, b0 + seg * nblk + h))

    return pl.pallas_call(
        functools.partial(_na_body, seq=seq),
        grid=(bsz, nblk),
        in_specs=[seg_spec(0), seg_spec(1), seg_spec(2),
                  pl.BlockSpec((hpb, NA_KH, GRID_W, NA_KH * GRID_W), lambda b, h: (h, 0, 0, 0))],
        out_specs=pl.BlockSpec((1, seq, LANES), lambda b, h: (b, 0, h)),
        out_shape=jax.ShapeDtypeStruct((bsz, seq, NA_HEADS * NA_DH), BF16),
        compiler_params=_cparams(("arbitrary", "arbitrary")),
        name="natten",
    )(proj, proj, proj, bt)


def _da_body(q_ref, k_ref, v_ref, lam_ref, g_ref, o_ref, *, lam_init):
    lp = lam_ref[...]
    lam = (jnp.exp(jnp.sum(lp[0:1] * lp[1:2], axis=-1, keepdims=True))
           - jnp.exp(jnp.sum(lp[2:3] * lp[3:4], axis=-1, keepdims=True)) + lam_init)
    scale = DA_DH ** -0.5

    def probs(lo):
        s = _dot_nt(q_ref[0, :, lo:lo + DA_DH], k_ref[0, :, lo:lo + DA_DH]) * scale
        e = jnp.exp(s - jnp.max(s, axis=-1, keepdims=True))
        return e, 1.0 / jnp.sum(e, axis=-1, keepdims=True)

    e1, r1 = probs(0)
    e2, r2 = probs(DA_DH)
    a = (e1 * r1 - e2 * (lam * r2)).astype(BF16)
    o = _dot(a, v_ref[0])
    o_ref[0] = (_rms(o) * g_ref[...] * (1.0 - lam_init)).astype(o_ref.dtype)


def _da(proj, diff_lambda, diff_norm, layer_idx, tq=256):
    bsz, seq, _ = proj.shape
    dv = 2 * DA_DH
    lam_init = 0.8 - 0.6 * math.exp(-0.3 * layer_idx)
    return pl.pallas_call(
        functools.partial(_da_body, lam_init=lam_init),
        grid=(bsz, DA_HEADS, seq // tq),
        in_specs=[pl.BlockSpec((1, tq, dv), lambda b, h, i: (b, i, h)),
                  pl.BlockSpec((1, seq, dv), lambda b, h, i: (b, 0, DA_HEADS + h)),
                  pl.BlockSpec((1, seq, dv), lambda b, h, i: (b, 0, 2 * DA_HEADS + h)),
                  pl.BlockSpec((4, DA_DH), lambda b, h, i: (0, 0)),
                  pl.BlockSpec((1, dv), lambda b, h, i: (0, 0))],
        out_specs=pl.BlockSpec((1, tq, dv), lambda b, h, i: (b, i, h)),
        out_shape=jax.ShapeDtypeStruct((bsz, seq, DA_HEADS * dv), BF16),
        compiler_params=_cparams(("arbitrary", "arbitrary", "arbitrary")),
        name="diff_attn",
    )(proj, proj, proj, diff_lambda, diff_norm.reshape(1, dv))


def _route_body(x_ref, g_ref, mod_ref, rwc_ref, rwh_ref, rb_ref,
                h_ref, e_ref, gate_ref, rank_ref, cnt_ref, run_s, *, tm):
    i = pl.program_id(0)

    @pl.when(i == 0)
    def _():
        run_s[...] = jnp.zeros_like(run_s)

    h = _norm_mod(x_ref[...], g_ref[...], mod_ref[0, 3:4, :], mod_ref[0, 4:5, :])
    h_ref[...] = h
    h_hi = h.astype(BF16)
    h_lo = (h - h_hi.astype(F32)).astype(BF16)
    both = _dot_nt(rwc_ref[...], h_hi)
    logits = both[0:N_EXPERTS] + both[N_EXPERTS:2 * N_EXPERTS] + _dot_nt(rwh_ref[...], h_lo)
    scores = jax.nn.sigmoid(logits)
    sel = scores + rb_ref[...]

    sc = [scores[e:e + 1, :] for e in range(N_EXPERTS)]
    sl = [sel[e:e + 1, :] for e in range(N_EXPERTS)]

    def top2_sum(a, b, c, d):
        hi1, lo1 = jnp.maximum(a, b), jnp.minimum(a, b)
        hi2, lo2 = jnp.maximum(c, d), jnp.minimum(c, d)
        return jnp.maximum(hi1, hi2) + jnp.maximum(jnp.minimum(hi1, hi2), jnp.maximum(lo1, lo2))

    gs = [top2_sum(*sl[4 * g:4 * g + 4]) for g in range(N_GROUPS)]
    best = jnp.zeros_like(gs[0], dtype=I32)
    best_v = gs[0]
    for g in range(1, N_GROUPS):
        better = gs[g] > best_v
        best = jnp.where(better, g, best)
        best_v = jnp.where(better, gs[g], best_v)

    def pick(vals, j):
        out = vals[j]
        for g in range(1, N_GROUPS):
            out = jnp.where(best == g, vals[4 * g + j], out)
        return out

    v = [pick(sl, j) for j in range(EXPERTS_PER_GROUP)]
    w = [pick(sc, j) for j in range(EXPERTS_PER_GROUP)]

    def argmax4(vals, excluded):
        idx = jnp.full_like(best, -1)
        cur = jnp.full_like(vals[0], -jnp.inf)
        for j in range(EXPERTS_PER_GROUP):
            ok = (vals[j] > cur) if excluded is None else ((vals[j] > cur) & (excluded != j))
            idx = jnp.where(ok, j, idx)
            cur = jnp.where(ok, vals[j], cur)
        return idx

    i1 = argmax4(v, None)
    i2 = argmax4(v, i1)

    def take(vals, idx):
        out = vals[0]
        for j in range(1, EXPERTS_PER_GROUP):
            out = jnp.where(idx == j, vals[j], out)
        return out

    w1, w2 = take(w, i1), take(w, i2)
    den = w1 + w2
    e1 = best * EXPERTS_PER_GROUP + i1
    e2 = best * EXPERTS_PER_GROUP + i2
    e_ref[0:1, :] = e1
    e_ref[1:2, :] = e2
    gate_ref[0:1, :] = w1 / den
    gate_ref[1:2, :] = w2 / den

    eid = lax.broadcasted_iota(I32, (N_EXPERTS, tm), 0)
    oh1 = eid == e1
    oh2 = eid == e2
    onehot = (oh1 | oh2).astype(BF16)
    srow = lax.broadcasted_iota(I32, (tm, tm), 0)
    tcol = lax.broadcasted_iota(I32, (tm, tm), 1)
    before = (srow < tcol).astype(BF16)
    pos = _dot(onehot, before) + run_s[...]
    rank_ref[0:1, :] = jnp.sum(jnp.where(oh1, pos, 0.0), axis=0, keepdims=True).astype(I32)
    rank_ref[1:2, :] = jnp.sum(jnp.where(oh2, pos, 0.0), axis=0, keepdims=True).astype(I32)
    run_s[...] = run_s[...] + jnp.sum(onehot.astype(F32), axis=1, keepdims=True)
    cnt_ref[...] = jnp.broadcast_to(run_s[...], cnt_ref.shape).astype(I32)


def _route(x, g, mod, router_w, router_b, seq, tm=512):
    n, d = x.shape
    tpb = seq // tm
    rwt = router_w.T
    rw_hi = rwt.astype(BF16)
    rw_lo = (rwt - rw_hi.astype(F32)).astype(BF16)
    rwc = jnp.concatenate([rw_hi, rw_lo], axis=0)
    pair_spec = pl.BlockSpec((2, tm), lambda i: (0, i))
    return pl.pallas_call(
        functools.partial(_route_body, tm=tm),
        grid=(n // tm,),
        in_specs=[pl.BlockSpec((tm, d), lambda i: (i, 0)),
                  pl.BlockSpec((1, d), lambda i: (0, 0)),
                  pl.BlockSpec((1, 6, d), lambda i: (i // tpb, 0, 0)),
                  pl.BlockSpec((2 * N_EXPERTS, d), lambda i: (0, 0)),
                  pl.BlockSpec((N_EXPERTS, d), lambda i: (0, 0)),
                  pl.BlockSpec((N_EXPERTS, 1), lambda i: (0, 0))],
        out_specs=[pl.BlockSpec((tm, d), lambda i: (i, 0)), pair_spec, pair_spec, pair_spec,
                   pl.BlockSpec((N_EXPERTS, LANES), lambda i: (0, 0))],
        out_shape=[jax.ShapeDtypeStruct((n, d), F32),
                   jax.ShapeDtypeStruct((2, n), I32),
                   jax.ShapeDtypeStruct((2, n), F32),
                   jax.ShapeDtypeStruct((2, n), I32),
                   jax.ShapeDtypeStruct((N_EXPERTS, LANES), I32)],
        scratch_shapes=[pltpu.VMEM((N_EXPERTS, 1), F32)],
        compiler_params=_cparams(("arbitrary",)),
        name="moe_route",
    )(x, g.reshape(1, d), mod, rwc, rw_hi, router_b.reshape(N_EXPERTS, 1).astype(F32))


def _row_copy(src_ref, s, dst_ref, d, sem):
    return pltpu.make_async_copy(src_ref.at[pl.ds(s, 1), :], dst_ref.at[pl.ds(d, 1), :], sem)


def _dispatch_body(meta_ref, dest_ref, h_ref, xs_ref, zero_s, sem, *, tmd, tm, max_tiles):
    i = pl.program_id(0)

    @pl.when(i == 0)
    def _():
        zero_s[...] = jnp.zeros_like(zero_s)
        for e in range(N_EXPERTS):
            @pl.when(meta_ref[N_EXPERTS + e] > 0)
            def _():
                start = pl.multiple_of(meta_ref[e] - tm, tm)
                cp = pltpu.make_async_copy(zero_s, xs_ref.at[pl.ds(start, tm), :], sem)
                cp.start()
                cp.wait()

        def fill_unused(t, carry):
            cp = pltpu.make_async_copy(zero_s, xs_ref.at[pl.ds(pl.multiple_of(t * tm, tm), tm), :], sem)
            cp.start()
            cp.wait()
            return carry

        lax.fori_loop(meta_ref[2 * N_EXPERTS], max_tiles, fill_unused, 0)

    def issue(t, carry):
        _row_copy(h_ref, t, xs_ref, dest_ref[0, 0, t], sem).start()
        _row_copy(h_ref, t, xs_ref, dest_ref[0, 0, tmd + t], sem).start()
        return carry

    lax.fori_loop(0, tmd, issue, 0)

    def drain(t, carry):
        _row_copy(h_ref, 0, xs_ref, 0, sem).wait()
        _row_copy(h_ref, 0, xs_ref, 0, sem).wait()
        return carry

    lax.fori_loop(0, tmd, drain, 0)


def _dispatch(h, dest, meta, n_slots, tm, tmd=256):
    n, d = h.shape
    dest_blk = jnp.concatenate([dest[0].reshape(n // tmd, 1, tmd), dest[1].reshape(n // tmd, 1, tmd)], axis=2)
    grid_spec = pltpu.PrefetchScalarGridSpec(
        num_scalar_prefetch=1,
        grid=(n // tmd,),
        in_specs=[pl.BlockSpec((1, 1, 2 * tmd), lambda i, m: (i, 0, 0), memory_space=pltpu.SMEM),
                  pl.BlockSpec((tmd, d), lambda i, m: (i, 0))],
        out_specs=pl.BlockSpec(memory_space=pl.ANY),
        scratch_shapes=[pltpu.VMEM((tm, d), F32), pltpu.SemaphoreType.DMA(())],
    )
    return pl.pallas_call(
        functools.partial(_dispatch_body, tmd=tmd, tm=tm, max_tiles=n_slots // tm),
        grid_spec=grid_spec,
        out_shape=jax.ShapeDtypeStruct((n_slots, d), F32),
        compiler_params=_cparams(("arbitrary",)),
        name="moe_dispatch",
    )(meta, dest_blk, h)


def _moe_up_body(te_ref, nv_ref, xs_ref, wg_ref, wu_ref, hid_ref, wgb_s, wub_s):
    i = pl.program_id(1)
    valid = i < nv_ref[0]
    changed = (i == 0) | (te_ref[i] != te_ref[jnp.maximum(i - 1, 0)])

    @pl.when(valid & changed)
    def _():
        wgb_s[...] = wg_ref[0].astype(BF16)
        wub_s[...] = wu_ref[0].astype(BF16)

    @pl.when(valid)
    def _():
        x = xs_ref[...].astype(BF16)
        a = _dot(x, wgb_s[...])
        u = _dot(x, wub_s[...])
        hid_ref[...] = (_silu(a) * u).astype(hid_ref.dtype)

    @pl.when(jnp.logical_not(valid))
    def _():
        hid_ref[...] = jnp.zeros_like(hid_ref)


def _moe_up(xs, wg, wu, te, nv, tm, fc=512):
    n_slots, d = xs.shape
    ff = wg.shape[2]
    max_tiles = n_slots // tm

    def row_map(j, i, te_r, nv_r):
        return (jnp.minimum(i, nv_r[0] - 1), 0)

    def w_map(j, i, te_r, nv_r):
        return (te_r[i], 0, j)

    grid_spec = pltpu.PrefetchScalarGridSpec(
        num_scalar_prefetch=2,
        grid=(ff // fc, max_tiles),
        in_specs=[pl.BlockSpec((tm, d), row_map),
                  pl.BlockSpec((1, d, fc), w_map),
                  pl.BlockSpec((1, d, fc), w_map)],
        out_specs=pl.BlockSpec((tm, fc), lambda j, i, te_r, nv_r: (i, j)),
        scratch_shapes=[pltpu.VMEM((d, fc), BF16), pltpu.VMEM((d, fc), BF16)],
    )
    return pl.pallas_call(
        _moe_up_body,
        grid_spec=grid_spec,
        out_shape=jax.ShapeDtypeStruct((n_slots, ff), BF16),
        compiler_params=_cparams(("arbitrary", "arbitrary")),
        name="moe_up",
    )(te, nv, xs, wg, wu)


def _moe_down_body(te_ref, nv_ref, hid_ref, wd_ref, o_ref, wdb_s):
    i = pl.program_id(1)
    valid = i < nv_ref[0]
    changed = (i == 0) | (te_ref[i] != te_ref[jnp.maximum(i - 1, 0)])

    @pl.when(valid & changed)
    def _():
        wdb_s[...] = wd_ref[0].astype(BF16)

    @pl.when(valid)
    def _():
        o_ref[...] = _dot(hid_ref[...], wdb_s[...])

    @pl.when(jnp.logical_not(valid))
    def _():
        o_ref[...] = jnp.zeros_like(o_ref)


def _moe_down(hid, wd, te, nv, tm, nc=1024):
    n_slots, ff = hid.shape
    d = wd.shape[2]
    max_tiles = n_slots // tm
    grid_spec = pltpu.PrefetchScalarGridSpec(
        num_scalar_prefetch=2,
        grid=(d // nc, max_tiles),
        in_specs=[pl.BlockSpec((tm, ff), lambda j, i, te_r, nv_r: (jnp.minimum(i, nv_r[0] - 1), 0)),
                  pl.BlockSpec((1, ff, nc), lambda j, i, te_r, nv_r: (te_r[i], 0, j))],
        out_specs=pl.BlockSpec((tm, nc), lambda j, i, te_r, nv_r: (i, j)),
        scratch_shapes=[pltpu.VMEM((ff, nc), BF16)],
    )
    return pl.pallas_call(
        _moe_down_body,
        grid_spec=grid_spec,
        out_shape=jax.ShapeDtypeStruct((n_slots, d), F32),
        compiler_params=_cparams(("arbitrary", "arbitrary")),
        name="moe_down",
    )(te, nv, hid, wd)


def _combine_body(dest_ref, x_ref, gate_ref, mod_ref, ys_ref, *rest, tc, final):
    if final:
        lnf_ref, o_ref, buf_s, sem = rest
    else:
        o_ref, buf_s, sem = rest

    def issue(t, carry):
        pltpu.make_async_copy(ys_ref.at[pl.ds(dest_ref[0, 0, t], 1), :], buf_s.at[0, pl.ds(t, 1), :], sem).start()
        pltpu.make_async_copy(ys_ref.at[pl.ds(dest_ref[0, 0, tc + t], 1), :], buf_s.at[1, pl.ds(t, 1), :], sem).start()
        return carry

    lax.fori_loop(0, tc, issue, 0)

    def drain(t, carry):
        pltpu.make_async_copy(ys_ref.at[pl.ds(0, 1), :], buf_s.at[0, pl.ds(0, 1), :], sem).wait()
        pltpu.make_async_copy(ys_ref.at[pl.ds(0, 1), :], buf_s.at[1, pl.ds(0, 1), :], sem).wait()
        return carry

    lax.fori_loop(0, tc, drain, 0)

    gates = gate_ref[...]
    y = x_ref[...] + mod_ref[0, 5:6, :] * (gates[:, 0:1] * buf_s[0] + gates[:, 1:2] * buf_s[1])
    if final:
        y = _rms(y) * lnf_ref[...]
    o_ref[...] = y


def _combine(x, ys, dest, gates, mod, seq, ln_final=None, tc=256):
    n, d = x.shape
    tpb = seq // tc
    final = ln_final is not None
    dest_blk = jnp.concatenate([dest[0].reshape(n // tc, 1, tc), dest[1].reshape(n // tc, 1, tc)], axis=2)
    in_specs = [pl.BlockSpec((1, 1, 2 * tc), lambda i: (i, 0, 0), memory_space=pltpu.SMEM),
                pl.BlockSpec((tc, d), lambda i: (i, 0)),
                pl.BlockSpec((tc, 2), lambda i: (i, 0)),
                pl.BlockSpec((1, 6, d), lambda i: (i // tpb, 0, 0)),
                pl.BlockSpec(memory_space=pl.ANY)]
    args = [dest_blk, x, gates.T, mod, ys]
    if final:
        in_specs.append(pl.BlockSpec((1, d), lambda i: (0, 0)))
        args.append(ln_final.reshape(1, d))
    return pl.pallas_call(
        functools.partial(_combine_body, tc=tc, final=final),
        grid=(n // tc,),
        in_specs=in_specs,
        out_specs=pl.BlockSpec((tc, d), lambda i: (i, 0)),
        out_shape=jax.ShapeDtypeStruct((n, d), F32),
        scratch_shapes=[pltpu.VMEM((2, tc, d), F32), pltpu.SemaphoreType.DMA(())],
        compiler_params=_cparams(("arbitrary",)),
        name="moe_combine",
    )(*args)


def _moe(x, ln_g, mod, router_w, router_b, wg, wu, wd, seq, ln_final=None, tm=512):
    n, d = x.shape
    h, eidx, gates, rank, cnt = _route(x, ln_g, mod, router_w, router_b, seq)
    counts = cnt[:, 0]
    ntile = (counts + tm - 1) // tm
    cum = jnp.cumsum(ntile)
    off = (cum - ntile) * tm
    max_tiles = (2 * n) // tm + N_EXPERTS
    nv = cum[-1:].astype(I32)
    tiles = jnp.minimum(jnp.arange(max_tiles, dtype=I32), nv[0] - 1)
    te = jnp.minimum(jnp.searchsorted(cum, tiles, side="right"), N_EXPERTS - 1).astype(I32)
    dest = (off[eidx] + rank).astype(I32)
    meta = jnp.concatenate([cum * tm, ntile, nv]).astype(I32)
    xs = _dispatch(h, dest, meta, max_tiles * tm, tm)
    hid = _moe_up(xs, wg, wu, te, nv, tm)
    ys = _moe_down(hid, wd, te, nv, tm)
    return _combine(x, ys, dest, gates, mod, seq, ln_final)


def kernel(x, c, ln_attn_0, ada_w_0, ada_b_0, w_in_0, hgrn_norm_0, na_rpb_0, w_out_0, ln_ffn_0, moe_wg_0, moe_wu_0, moe_wd_0, ln_attn_1, ada_w_1, ada_b_1, w_in_1, diff_lambda_1, diff_norm_1, w_out_1, ln_ffn_1, moe_wg_1, moe_wu_1, moe_wd_1, hgrn_lb_logits, router_w, router_b, ln_final):
    bsz, seq, d = x.shape
    n = bsz * seq
    xf = x.reshape(n, d)
    mod0 = _ada(c, ada_w_0, ada_b_0)
    mod1 = _ada(c, ada_w_1, ada_b_1)

    h = _norm(xf, ln_attn_0, mod0, 0, 1, seq)
    proj = _mm(h, w_in_0.astype(BF16)).reshape(bsz, seq, -1)
    o_a = _hgrn(proj, hgrn_lb_logits, hgrn_norm_0)
    o_b = _na(proj, na_rpb_0, 5 * HG_HEADS * HG_D)
    xf = _mm_res([o_a.reshape(n, -1), o_b.reshape(n, -1)], w_out_0.astype(BF16), xf, mod0, 2, seq)
    xf = _moe(xf, ln_ffn_0, mod0, router_w, router_b, moe_wg_0, moe_wu_0, moe_wd_0, seq)

    h = _norm(xf, ln_attn_1, mod1, 0, 1, seq)
    proj = _mm_rope(h, w_in_1.astype(BF16), seq, 4 * DA_HEADS * DA_DH).reshape(bsz, seq, -1)
    o_c = _da(proj, diff_lambda_1, diff_norm_1, 1)
    xf = _mm_res([o_c.reshape(n, -1)], w_out_1.astype(BF16), xf, mod1, 2, seq)
    xf = _moe(xf, ln_ffn_1, mod1, router_w, router_b, moe_wg_1, moe_wu_1, moe_wd_1, seq, ln_final=ln_final)
    return xf.reshape(bsz, seq, d)
```

```python
import functools
import math

import jax
import jax.numpy as jnp
from jax import lax
from jax.experimental import pallas as pl
from jax.experimental.pallas import tpu as pltpu

F32 = jnp.float32
BF16 = jnp.bfloat16
I32 = jnp.int32

EPS = 1e-6
LANES = 128
GRID_W = 64
HG_HEADS = 8
HG_D = 128
HG_CHUNK = 32
HG_BLK = 256
HG_SBLK = 128
NA_HEADS = 16
NA_DH = 64
NA_KH = 8
NA_KW = 16
NA_ROWS_PER_STEP = 4
DA_HEADS = 8
DA_DH = 128
DA_ROW_PIECES = 2
ROPE_THETA = 500000.0
ROPE_DIM = DA_DH // 4
N_EXPERTS = 16
N_GROUPS = 4
EXPERTS_PER_GROUP = 4
NEG_BIG = -1e30
ZERO_ROWS = 64

VMEM_LIMIT = 56 * 1024 * 1024


def _cparams(sem):
    return pltpu.CompilerParams(dimension_semantics=sem, vmem_limit_bytes=VMEM_LIMIT)


def _dot(a, b):
    return jnp.dot(a, b, preferred_element_type=F32)


def _dot_nt(a, b):
    return lax.dot_general(a, b, (((1,), (1,)), ((), ())), preferred_element_type=F32)


def _silu(x):
    return x * jax.nn.sigmoid(x)


def _rms(x):
    return x * lax.rsqrt(jnp.mean(x * x, axis=-1, keepdims=True) + EPS)


def _ada_body(c_ref, w_ref, b_ref, o_ref):
    a = _silu(c_ref[...]).astype(BF16)
    o_ref[...] = _dot(a, w_ref[...].astype(BF16)) + b_ref[...]


def _ada(c, w, b):
    bsz, d = c.shape
    n = w.shape[1]
    rows = 16
    c16 = jnp.concatenate([c, jnp.zeros((rows - bsz, d), c.dtype)], axis=0)
    tn = 1024
    out = pl.pallas_call(
        _ada_body,
        grid=(n // tn,),
        in_specs=[pl.BlockSpec((rows, d), lambda j: (0, 0)),
                  pl.BlockSpec((d, tn), lambda j: (0, j)),
                  pl.BlockSpec((1, tn), lambda j: (0, j))],
        out_specs=pl.BlockSpec((rows, tn), lambda j: (0, j)),
        out_shape=jax.ShapeDtypeStruct((rows, n), F32),
        compiler_params=_cparams(("arbitrary",)),
        name="ada_mod",
    )(c16, w, b.reshape(1, n))
    return out[:bsz].reshape(bsz, 6, d)


def _norm_mod(x, g, sh, sc):
    return (_rms(x) * g) * (1.0 + sc) + sh


def _norm_body(x_ref, g_ref, mod_ref, o_ref, *, sh_idx, sc_idx):
    h = _norm_mod(x_ref[...], g_ref[...], mod_ref[0, sh_idx:sh_idx + 1, :], mod_ref[0, sc_idx:sc_idx + 1, :])
    o_ref[...] = h.astype(o_ref.dtype)


def _norm(x, g, mod, sh_idx, sc_idx, seq, tm=512):
    n, d = x.shape
    tpb = seq // tm
    return pl.pallas_call(
        functools.partial(_norm_body, sh_idx=sh_idx, sc_idx=sc_idx),
        grid=(n // tm,),
        in_specs=[pl.BlockSpec((tm, d), lambda i: (i, 0)),
                  pl.BlockSpec((1, d), lambda i: (0, 0)),
                  pl.BlockSpec((1, 6, d), lambda i: (i // tpb, 0, 0))],
        out_specs=pl.BlockSpec((tm, d), lambda i: (i, 0)),
        out_shape=jax.ShapeDtypeStruct((n, d), BF16),
        compiler_params=_cparams(("arbitrary",)),
        name="norm_mod",
    )(x, g.reshape(1, d), mod)


def _mm_body(a_ref, w_ref, o_ref):
    o_ref[...] = _dot(a_ref[...], w_ref[...]).astype(o_ref.dtype)


def _mm(a, w, tm=512, tn=1024):
    n, k = a.shape
    nout = w.shape[1]
    return pl.pallas_call(
        _mm_body,
        grid=(n // tm, nout // tn),
        in_specs=[pl.BlockSpec((tm, k), lambda i, j: (i, 0)),
                  pl.BlockSpec((k, tn), lambda i, j: (0, j))],
        out_specs=pl.BlockSpec((tm, tn), lambda i, j: (i, j)),
        out_shape=jax.ShapeDtypeStruct((n, nout), BF16),
        compiler_params=_cparams(("arbitrary", "arbitrary")),
        name="proj_in",
    )(a, w)


def _mm_rope_body(a_ref, w_ref, cos_ref, sa_ref, sb_ref, o_ref, *, n_rope_tiles, tn):
    acc = _dot(a_ref[...], w_ref[...])
    j = pl.program_id(1)

    @pl.when(j < n_rope_tiles)
    def _():
        cos, sa, sb = cos_ref[...], sa_ref[...], sb_ref[...]
        half = ROPE_DIM // 2
        for c in range(tn // LANES):
            seg = acc[:, c * LANES:(c + 1) * LANES]
            r = seg * cos + pltpu.roll(seg, LANES - half, 1) * sa + pltpu.roll(seg, half, 1) * sb
            o_ref[:, c * LANES:(c + 1) * LANES] = r.astype(o_ref.dtype)

    @pl.when(j >= n_rope_tiles)
    def _():
        o_ref[...] = acc.astype(o_ref.dtype)


def _rope_tables(seq):
    half = ROPE_DIM // 2
    inv = ROPE_THETA ** (-jnp.arange(half, dtype=F32) / half)
    ang = jnp.arange(seq, dtype=F32)[:, None] * inv[None, :]
    cos, sin = jnp.cos(ang), jnp.sin(ang)
    pad = jnp.zeros((seq, LANES - ROPE_DIM), F32)
    z = jnp.zeros((seq, half), F32)
    cos_t = jnp.concatenate([cos, cos, pad + 1.0], axis=1)
    sa_t = jnp.concatenate([-sin, z, pad], axis=1)
    sb_t = jnp.concatenate([z, sin, pad], axis=1)
    return cos_t, sa_t, sb_t


def _mm_rope(a, w, seq, rope_cols, tm=512, tn=1024):
    n, k = a.shape
    nout = w.shape[1]
    tpb = seq // tm
    cos_t, sa_t, sb_t = _rope_tables(seq)
    tab_spec = pl.BlockSpec((tm, LANES), lambda i, j: (i % tpb, 0))
    return pl.pallas_call(
        functools.partial(_mm_rope_body, n_rope_tiles=rope_cols // tn, tn=tn),
        grid=(n // tm, nout // tn),
        in_specs=[pl.BlockSpec((tm, k), lambda i, j: (i, 0)),
                  pl.BlockSpec((k, tn), lambda i, j: (0, j)),
                  tab_spec, tab_spec, tab_spec],
        out_specs=pl.BlockSpec((tm, tn), lambda i, j: (i, j)),
        out_shape=jax.ShapeDtypeStruct((n, nout), BF16),
        compiler_params=_cparams(("arbitrary", "arbitrary")),
        name="proj_in_rope",
    )(a, w, cos_t, sa_t, sb_t)


def _mm_res_body(*refs, n_a, gt_idx):
    a_refs = refs[:n_a]
    w_ref, x_ref, mod_ref, o_ref = refs[n_a:]
    acc = None
    k0 = 0
    for a_ref in a_refs:
        kk = a_ref.shape[1]
        part = _dot(a_ref[...], w_ref[k0:k0 + kk, :])
        acc = part if acc is None else acc + part
        k0 += kk
    o_ref[...] = x_ref[...] + mod_ref[0, gt_idx:gt_idx + 1, :] * acc


def _mm_res(a_list, w, x, mod, gt_idx, seq, tm=512, tn=1024):
    n, d = x.shape
    tpb = seq // tm
    in_specs = [pl.BlockSpec((tm, a.shape[1]), lambda i, j: (i, 0)) for a in a_list]
    in_specs += [pl.BlockSpec((w.shape[0], tn), lambda i, j: (0, j)),
                 pl.BlockSpec((tm, tn), lambda i, j: (i, j)),
                 pl.BlockSpec((1, 6, tn), lambda i, j: (i // tpb, 0, j))]
    return pl.pallas_call(
        functools.partial(_mm_res_body, n_a=len(a_list), gt_idx=gt_idx),
        grid=(n // tm, d // tn),
        in_specs=in_specs,
        out_specs=pl.BlockSpec((tm, tn), lambda i, j: (i, j)),
        out_shape=jax.ShapeDtypeStruct((n, d), F32),
        compiler_params=_cparams(("arbitrary", "arbitrary")),
        name="proj_out_res",
    )(*a_list, w, x, mod)


def _chunk_prefix(g):
    r = g.shape[0]
    sub_rows = 8
    per_chunk = HG_CHUNK // sub_rows
    x = g.reshape(r // sub_rows, sub_rows, HG_D)
    sub = lax.broadcasted_iota(I32, x.shape, 1)
    for s in (1, 2, 4):
        x = x + jnp.where(sub >= s, pltpu.roll(x, s, 1), 0.0)
    tot = x[:, sub_rows - 1:sub_rows, :].reshape(r // HG_CHUNK, per_chunk, 1, HG_D)
    c1 = tot[:, 0:1]
    c2 = c1 + tot[:, 1:2]
    c3 = c2 + tot[:, 2:3]
    carry = jnp.concatenate([jnp.zeros_like(c1), c1, c2, c3], axis=1)
    return x.reshape(r // HG_CHUNK, per_chunk, sub_rows, HG_D) + carry, c3 + tot[:, 3:4]


def _hgrn_body(q_ref, i_ref, ff_ref, fb_ref, gg_ref, lbl_ref, ng_ref, o_ref,
               qg_s, kd_s, dec_s, vt_s, oacc_s, *, seq):
    blk = HG_BLK
    sblk = HG_SBLK
    cpb = blk // HG_CHUNK
    cps = sblk // HG_CHUNK
    lg = lbl_ref[...]
    ex = jnp.exp(lg - jnp.max(lg, axis=0, keepdims=True))
    lb = ex[0:1, :] / jnp.sum(ex, axis=0, keepdims=True)

    row = lax.broadcasted_iota(I32, (blk, blk), 0)
    col = lax.broadcasted_iota(I32, (blk, blk), 1)
    same_chunk = (row // HG_CHUNK) == (col // HG_CHUNK)
    masks = (same_chunk & (row >= col), same_chunk & (row <= col))
    srow = lax.broadcasted_iota(I32, (sblk, HG_D), 0) // HG_CHUNK
    f_refs = (ff_ref, fb_ref)
    shape4 = (cpb, HG_CHUNK // 8, 8, HG_D)

    def stage1(b, carry):
        sl = pl.ds(pl.multiple_of(b * blk, blk), blk)
        qh = _silu(q_ref[0, sl, :].astype(F32)).reshape(shape4)
        v = i_ref[0, sl, :]
        vf = v.astype(F32)
        for half in range(blk // sblk):
            vt_s[b * (blk // sblk) + half] = vf[half * sblk:(half + 1) * sblk, :].T.astype(BF16)
        o_blk = None
        for d in range(2):
            f = lb + (1.0 - lb) * jax.nn.sigmoid(f_refs[d][0, sl, :].astype(F32))
            g = jnp.log(f)
            k = (1.0 - f).reshape(shape4)
            pref, total = _chunk_prefix(g)
            gc = pref if d == 0 else total - pref + g.reshape(shape4)
            eg = jnp.exp(gc)
            dec = jnp.exp(total)
            qg = (qh * eg).reshape(blk, HG_D).astype(BF16)
            kgf = k * (1.0 / eg)
            kg = kgf.reshape(blk, HG_D).astype(BF16)
            qg_s[d, sl, :] = qg
            kd_s[d, sl, :] = (kgf * dec).reshape(blk, HG_D).astype(BF16)
            dec_s[d, pl.ds(pl.multiple_of(b * cpb, cpb), cpb)] = dec.reshape(cpb, 1, HG_D)
            a = jnp.where(masks[d], _dot_nt(qg, kg), 0.0).astype(BF16)
            o = _dot(a, v)
            o_blk = o if o_blk is None else o_blk + o
        oacc_s[sl, :] = o_blk
        return carry

    lax.fori_loop(0, seq // blk, stage1, 0)

    n_sblk = seq // sblk

    def stage2(j, carry):
        states = list(carry)
        sbs = (j, n_sblk - 1 - j)
        uts = []
        for d in range(2):
            vt_b = vt_s[sbs[d]]
            kd_b = kd_s[d, pl.ds(pl.multiple_of(sbs[d] * sblk, sblk), sblk), :]
            uts.append([_dot(vt_b, jnp.where(srow == jj, kd_b, jnp.zeros_like(kd_b))) for jj in range(cps)])
        for step in range(cps):
            for d in range(2):
                jj = step if d == 0 else cps - 1 - step
                rows = pl.ds(pl.multiple_of(sbs[d] * sblk + jj * HG_CHUNK, HG_CHUNK), HG_CHUNK)
                oacc_s[rows, :] += _dot_nt(qg_s[d, rows, :], states[d].astype(BF16))
                states[d] = states[d] * dec_s[d, sbs[d] * cps + jj] + uts[d][jj]
        return tuple(states)

    zero = jnp.zeros((HG_D, HG_D), F32)
    lax.fori_loop(0, n_sblk, stage2, (zero, zero))

    y = _rms(oacc_s[...]) * ng_ref[0]
    o_ref[0] = (y * _silu(gg_ref[0].astype(F32))).astype(o_ref.dtype)


def _hgrn(proj, lb_logits, norm_g):
    bsz, seq, _ = proj.shape
    n_chunks = seq // HG_CHUNK
    n_layers = lb_logits.shape[0]

    def seg_spec(seg):
        return pl.BlockSpec((1, seq, HG_D), lambda b, h: (b, 0, seg * HG_HEADS + h))

    return pl.pallas_call(
        functools.partial(_hgrn_body, seq=seq),
        grid=(bsz, HG_HEADS),
        in_specs=[seg_spec(0), seg_spec(1), seg_spec(2), seg_spec(3), seg_spec(4),
                  pl.BlockSpec((n_layers, HG_D), lambda b, h: (0, h)),
                  pl.BlockSpec((1, 1, HG_D), lambda b, h: (h, 0, 0))],
        out_specs=pl.BlockSpec((1, seq, HG_D), lambda b, h: (b, 0, h)),
        out_shape=jax.ShapeDtypeStruct((bsz, seq, HG_HEADS * HG_D), BF16),
        scratch_shapes=[pltpu.VMEM((2, seq, HG_D), BF16), pltpu.VMEM((2, seq, HG_D), BF16),
                        pltpu.VMEM((2, n_chunks, 1, HG_D), F32),
                        pltpu.VMEM((seq // HG_SBLK, HG_D, HG_SBLK), BF16),
                        pltpu.VMEM((seq, HG_D), F32)],
        compiler_params=_cparams(("arbitrary", "arbitrary")),
        name="hgrn2",
    )(proj, proj, proj, proj, proj, lb_logits, norm_g.reshape(HG_HEADS, 1, HG_D))


def _na_bias_table(rpb):
    col = jnp.arange(GRID_W)
    dc = jnp.clip(col[None, :] - col[:, None] + (NA_KW - 1), 0, 2 * NA_KW - 2)
    sel = (dc[None] == jnp.arange(2 * NA_KW - 1)[:, None, None]).astype(F32)
    toe = jnp.einsum("hrd,dqk->hrqk", rpb.astype(F32), sel, precision=lax.Precision.HIGHEST)
    pats = []
    for p in range(NA_KH):
        w = toe[:, NA_KH - 1 - p:2 * NA_KH - 1 - p]
        pats.append(w.transpose(0, 2, 1, 3).reshape(rpb.shape[0], GRID_W, NA_KH * GRID_W))
    return jnp.stack(pats, axis=1)


def _na_body(q_ref, k_ref, v_ref, bt_ref, o_ref, *, seq):
    rows = seq // GRID_W
    win = NA_KH * GRID_W
    cq = lax.broadcasted_iota(I32, (GRID_W, win), 0)
    ck = lax.broadcasted_iota(I32, (GRID_W, win), 1) % GRID_W
    cs = jnp.clip(cq - NA_KW // 2, 0, GRID_W - NA_KW)
    in_win = (ck >= cs) & (ck < cs + NA_KW)
    scale = NA_DH ** -0.5
    heads_here = LANES // NA_DH
    lane_head = lax.broadcasted_iota(I32, (GRID_W, LANES), 1) // NA_DH
    out_head = lane_head

    def row_group(rg, carry):
        items = []
        for u in range(NA_ROWS_PER_STEP):
            r = rg * NA_ROWS_PER_STEP + u
            start = jnp.clip(r - NA_KH // 2, 0, rows - NA_KH)
            qs = pl.ds(pl.multiple_of(r * GRID_W, GRID_W), GRID_W)
            ks = pl.ds(pl.multiple_of(start * GRID_W, GRID_W), win)
            q = q_ref[0, qs, :]
            k = k_ref[0, ks, :]
            for hh in range(heads_here):
                qh = jnp.where(lane_head == hh, q, jnp.zeros_like(q))
                items.append((_dot_nt(qh, k), hh, r - start, qs, ks))
        probs = []
        for s, hh, pat, qs, ks in items:
            s = jnp.where(in_win, s * scale + bt_ref[hh, pat], NEG_BIG)
            e = jnp.exp(s - jnp.max(s, axis=-1, keepdims=True))
            probs.append((e.astype(BF16), 1.0 / jnp.sum(e, axis=-1, keepdims=True), qs, ks))
        for i in range(0, len(probs), heads_here):
            qs, ks = probs[i][2], probs[i][3]
            v = v_ref[0, ks, :]
            o = None
            for hh in range(heads_here):
                oh = _dot(probs[i + hh][0], v) * probs[i + hh][1]
                o = oh if o is None else jnp.where(out_head == hh, oh, o)
            o_ref[0, qs, :] = o.astype(o_ref.dtype)
        return carry

    lax.fori_loop(0, rows // NA_ROWS_PER_STEP, row_group, 0)


def _na(proj, rpb, col0):
    bsz, seq, _ = proj.shape
    hpb = LANES // NA_DH
    nblk = NA_HEADS // hpb
    b0 = col0 // LANES
    bt = _na_bias_table(rpb)

    def seg_spec(seg):
        return pl.BlockSpec((1, seq, LANES), lambda b, h: (b, 0, b0 + seg * nblk + h))

    return pl.pallas_call(
        functools.partial(_na_body, seq=seq),
        grid=(bsz, nblk),
        in_specs=[seg_spec(0), seg_spec(1), seg_spec(2),
                  pl.BlockSpec((hpb, NA_KH, GRID_W, NA_KH * GRID_W), lambda b, h: (h, 0, 0, 0))],
        out_specs=pl.BlockSpec((1, seq, LANES), lambda b, h: (b, 0, h)),
        out_shape=jax.ShapeDtypeStruct((bsz, seq, NA_HEADS * NA_DH), BF16),
        compiler_params=_cparams(("arbitrary", "arbitrary")),
        name="natten",
    )(proj, proj, proj, bt)


def _da_body(q_ref, k_ref, v_ref, lam_ref, g_ref, o_ref, *, lam_init):
    lp = lam_ref[...]
    lam = (jnp.exp(jnp.sum(lp[0:1] * lp[1:2], axis=-1, keepdims=True))
           - jnp.exp(jnp.sum(lp[2:3] * lp[3:4], axis=-1, keepdims=True)) + lam_init)
    scale2 = (DA_DH ** -0.5) * math.log2(math.e)
    v = v_ref[0]

    def weighted(s):
        e = jnp.exp2(s * scale2 - jnp.max(s, axis=-1, keepdims=True) * scale2)
        return _dot(e.astype(BF16), v), 1.0 / jnp.sum(e, axis=-1, keepdims=True)

    rows_per = q_ref.shape[1] // DA_ROW_PIECES
    pieces = [slice(p * rows_per, (p + 1) * rows_per) for p in range(DA_ROW_PIECES)]
    ss = [[_dot_nt(q_ref[0, rs, lo:lo + DA_DH], k_ref[0, :, lo:lo + DA_DH]) for lo in (0, DA_DH)]
          for rs in pieces]
    for rs, (s1, s2) in zip(pieces, ss):
        o1, r1 = weighted(s1)
        o2, r2 = weighted(s2)
        o = o1 * r1 - o2 * (lam * r2)
        o_ref[0, rs, :] = (_rms(o) * g_ref[...] * (1.0 - lam_init)).astype(o_ref.dtype)


def _da(proj, diff_lambda, diff_norm, layer_idx, tq=256):
    bsz, seq, _ = proj.shape
    dv = 2 * DA_DH
    lam_init = 0.8 - 0.6 * math.exp(-0.3 * layer_idx)
    return pl.pallas_call(
        functools.partial(_da_body, lam_init=lam_init),
        grid=(bsz, DA_HEADS, seq // tq),
        in_specs=[pl.BlockSpec((1, tq, dv), lambda b, h, i: (b, i, h)),
                  pl.BlockSpec((1, seq, dv), lambda b, h, i: (b, 0, DA_HEADS + h)),
                  pl.BlockSpec((1, seq, dv), lambda b, h, i: (b, 0, 2 * DA_HEADS + h)),
                  pl.BlockSpec((4, DA_DH), lambda b, h, i: (0, 0)),
                  pl.BlockSpec((1, dv), lambda b, h, i: (0, 0))],
        out_specs=pl.BlockSpec((1, tq, dv), lambda b, h, i: (b, i, h)),
        out_shape=jax.ShapeDtypeStruct((bsz, seq, DA_HEADS * dv), BF16),
        compiler_params=_cparams(("arbitrary", "arbitrary", "arbitrary")),
        name="diff_attn",
    )(proj, proj, proj, diff_lambda, diff_norm.reshape(1, dv))


def _route_body(x_ref, g_ref, mod_ref, rwc_ref, rwh_ref, rb_ref,
                h_ref, e_ref, gate_ref, rank_ref, cnt_ref, run_s, *, tm):
    i = pl.program_id(0)

    @pl.when(i == 0)
    def _():
        run_s[...] = jnp.zeros_like(run_s)

    h = _norm_mod(x_ref[...], g_ref[...], mod_ref[0, 3:4, :], mod_ref[0, 4:5, :])
    h_ref[...] = h
    h_hi = h.astype(BF16)
    h_lo = (h - h_hi.astype(F32)).astype(BF16)
    both = _dot_nt(rwc_ref[...], h_hi)
    logits = both[0:N_EXPERTS] + both[N_EXPERTS:2 * N_EXPERTS] + _dot_nt(rwh_ref[...], h_lo)
    scores = jax.nn.sigmoid(logits)
    sel = scores + rb_ref[...]

    sc = [scores[e:e + 1, :] for e in range(N_EXPERTS)]
    sl = [sel[e:e + 1, :] for e in range(N_EXPERTS)]

    def top2_sum(a, b, c, d):
        hi1, lo1 = jnp.maximum(a, b), jnp.minimum(a, b)
        hi2, lo2 = jnp.maximum(c, d), jnp.minimum(c, d)
        return jnp.maximum(hi1, hi2) + jnp.maximum(jnp.minimum(hi1, hi2), jnp.maximum(lo1, lo2))

    gs = [top2_sum(*sl[4 * g:4 * g + 4]) for g in range(N_GROUPS)]
    best = jnp.zeros_like(gs[0], dtype=I32)
    best_v = gs[0]
    for g in range(1, N_GROUPS):
        better = gs[g] > best_v
        best = jnp.where(better, g, best)
        best_v = jnp.where(better, gs[g], best_v)

    def pick(vals, j):
        out = vals[j]
        for g in range(1, N_GROUPS):
            out = jnp.where(best == g, vals[4 * g + j], out)
        return out

    v = [pick(sl, j) for j in range(EXPERTS_PER_GROUP)]
    w = [pick(sc, j) for j in range(EXPERTS_PER_GROUP)]

    def argmax4(vals, excluded):
        idx = jnp.full_like(best, -1)
        cur = jnp.full_like(vals[0], -jnp.inf)
        for j in range(EXPERTS_PER_GROUP):
            ok = (vals[j] > cur) if excluded is None else ((vals[j] > cur) & (excluded != j))
            idx = jnp.where(ok, j, idx)
            cur = jnp.where(ok, vals[j], cur)
        return idx

    i1 = argmax4(v, None)
    i2 = argmax4(v, i1)

    def take(vals, idx):
        out = vals[0]
        for j in range(1, EXPERTS_PER_GROUP):
            out = jnp.where(idx == j, vals[j], out)
        return out

    w1, w2 = take(w, i1), take(w, i2)
    den = w1 + w2
    e1 = best * EXPERTS_PER_GROUP + i1
    e2 = best * EXPERTS_PER_GROUP + i2
    e_ref[0:1, :] = e1
    e_ref[1:2, :] = e2
    gate_ref[0:1, :] = w1 / den
    gate_ref[1:2, :] = w2 / den

    eid = lax.broadcasted_iota(I32, (N_EXPERTS, tm), 0)
    oh1 = eid == e1
    oh2 = eid == e2
    onehot = (oh1 | oh2).astype(BF16)
    srow = lax.broadcasted_iota(I32, (tm, tm), 0)
    tcol = lax.broadcasted_iota(I32, (tm, tm), 1)
    before = (srow < tcol).astype(BF16)
    pos = _dot(onehot, before) + run_s[...]
    rank_ref[0:1, :] = jnp.sum(jnp.where(oh1, pos, 0.0), axis=0, keepdims=True).astype(I32)
    rank_ref[1:2, :] = jnp.sum(jnp.where(oh2, pos, 0.0), axis=0, keepdims=True).astype(I32)
    run_s[...] = run_s[...] + jnp.sum(onehot.astype(F32), axis=1, keepdims=True)
    cnt_ref[...] = jnp.broadcast_to(run_s[...], cnt_ref.shape).astype(I32)


def _route(x, g, mod, router_w, router_b, seq, tm=512):
    n, d = x.shape
    tpb = seq // tm
    rwt = router_w.T
    rw_hi = rwt.astype(BF16)
    rw_lo = (rwt - rw_hi.astype(F32)).astype(BF16)
    rwc = jnp.concatenate([rw_hi, rw_lo], axis=0)
    pair_spec = pl.BlockSpec((2, tm), lambda i: (0, i))
    return pl.pallas_call(
        functools.partial(_route_body, tm=tm),
        grid=(n // tm,),
        in_specs=[pl.BlockSpec((tm, d), lambda i: (i, 0)),
                  pl.BlockSpec((1, d), lambda i: (0, 0)),
                  pl.BlockSpec((1, 6, d), lambda i: (i // tpb, 0, 0)),
                  pl.BlockSpec((2 * N_EXPERTS, d), lambda i: (0, 0)),
                  pl.BlockSpec((N_EXPERTS, d), lambda i: (0, 0)),
                  pl.BlockSpec((N_EXPERTS, 1), lambda i: (0, 0))],
        out_specs=[pl.BlockSpec((tm, d), lambda i: (i, 0)), pair_spec, pair_spec, pair_spec,
                   pl.BlockSpec((N_EXPERTS, LANES), lambda i: (0, 0))],
        out_shape=[jax.ShapeDtypeStruct((n, d), F32),
                   jax.ShapeDtypeStruct((2, n), I32),
                   jax.ShapeDtypeStruct((2, n), F32),
                   jax.ShapeDtypeStruct((2, n), I32),
                   jax.ShapeDtypeStruct((N_EXPERTS, LANES), I32)],
        scratch_shapes=[pltpu.VMEM((N_EXPERTS, 1), F32)],
        compiler_params=_cparams(("arbitrary",)),
        name="moe_route",
    )(x, g.reshape(1, d), mod, rwc, rw_hi, router_b.reshape(N_EXPERTS, 1).astype(F32))


def _row_copy(src_ref, s, dst_ref, d, sem):
    return pltpu.make_async_copy(src_ref.at[pl.ds(s, 1), :], dst_ref.at[pl.ds(d, 1), :], sem)


def _dispatch_body(meta_ref, dest_ref, h_ref, xs_ref, zero_s, sem, *, tmd, tm, max_tiles):
    i = pl.program_id(0)
    zr = zero_s.shape[0]

    def zero_tile(start):
        copies = [pltpu.make_async_copy(zero_s, xs_ref.at[pl.ds(pl.multiple_of(start + z * zr, zr), zr), :], sem)
                  for z in range(tm // zr)]
        for cp in copies:
            cp.start()
        for cp in copies:
            cp.wait()

    @pl.when(i == 0)
    def _():
        zero_s[...] = jnp.zeros_like(zero_s)
        for e in range(N_EXPERTS):
            @pl.when(meta_ref[N_EXPERTS + e] > 0)
            def _():
                zero_tile(pl.multiple_of(meta_ref[e] - tm, tm))

        def fill_unused(t, carry):
            zero_tile(pl.multiple_of(t * tm, tm))
            return carry

        lax.fori_loop(meta_ref[2 * N_EXPERTS], max_tiles, fill_unused, 0)

    def issue(t, carry):
        _row_copy(h_ref, t, xs_ref, dest_ref[0, 0, t], sem).start()
        _row_copy(h_ref, t, xs_ref, dest_ref[0, 0, tmd + t], sem).start()
        return carry

    lax.fori_loop(0, tmd, issue, 0)

    def drain(t, carry):
        _row_copy(h_ref, 0, xs_ref, 0, sem).wait()
        _row_copy(h_ref, 0, xs_ref, 0, sem).wait()
        return carry

    lax.fori_loop(0, tmd, drain, 0)


def _dispatch(h, dest, meta, n_slots, tm, tmd=256):
    n, d = h.shape
    dest_blk = jnp.concatenate([dest[0].reshape(n // tmd, 1, tmd), dest[1].reshape(n // tmd, 1, tmd)], axis=2)
    grid_spec = pltpu.PrefetchScalarGridSpec(
        num_scalar_prefetch=1,
        grid=(n // tmd,),
        in_specs=[pl.BlockSpec((1, 1, 2 * tmd), lambda i, m: (i, 0, 0), memory_space=pltpu.SMEM),
                  pl.BlockSpec((tmd, d), lambda i, m: (i, 0))],
        out_specs=pl.BlockSpec(memory_space=pl.ANY),
        scratch_shapes=[pltpu.VMEM((ZERO_ROWS, d), F32), pltpu.SemaphoreType.DMA(())],
    )
    return pl.pallas_call(
        functools.partial(_dispatch_body, tmd=tmd, tm=tm, max_tiles=n_slots // tm),
        grid_spec=grid_spec,
        out_shape=jax.ShapeDtypeStruct((n_slots, d), F32),
        compiler_params=_cparams(("arbitrary",)),
        name="moe_dispatch",
    )(meta, dest_blk, h)


def _moe_up_body(te_ref, nv_ref, xs_ref, wg_ref, wu_ref, hid_ref, wgb_s, wub_s):
    i = pl.program_id(1)
    valid = i < nv_ref[0]
    changed = (i == 0) | (te_ref[i] != te_ref[jnp.maximum(i - 1, 0)])

    @pl.when(valid & changed)
    def _():
        wgb_s[...] = wg_ref[0].astype(BF16)
        wub_s[...] = wu_ref[0].astype(BF16)

    @pl.when(valid)
    def _():
        x = xs_ref[...].astype(BF16)
        a = _dot(x, wgb_s[...])
        u = _dot(x, wub_s[...])
        hid_ref[...] = (_silu(a) * u).astype(hid_ref.dtype)

    @pl.when(jnp.logical_not(valid))
    def _():
        hid_ref[...] = jnp.zeros_like(hid_ref)


def _moe_up(xs, wg, wu, te, nv, tm, fc=512):
    n_slots, d = xs.shape
    ff = wg.shape[2]
    max_tiles = n_slots // tm

    def row_map(j, i, te_r, nv_r):
        return (jnp.minimum(i, nv_r[0] - 1), 0)

    def w_map(j, i, te_r, nv_r):
        return (te_r[i], 0, j)

    grid_spec = pltpu.PrefetchScalarGridSpec(
        num_scalar_prefetch=2,
        grid=(ff // fc, max_tiles),
        in_specs=[pl.BlockSpec((tm, d), row_map),
                  pl.BlockSpec((1, d, fc), w_map),
                  pl.BlockSpec((1, d, fc), w_map)],
        out_specs=pl.BlockSpec((tm, fc), lambda j, i, te_r, nv_r: (i, j)),
        scratch_shapes=[pltpu.VMEM((d, fc), BF16), pltpu.VMEM((d, fc), BF16)],
    )
    return pl.pallas_call(
        _moe_up_body,
        grid_spec=grid_spec,
        out_shape=jax.ShapeDtypeStruct((n_slots, ff), BF16),
        compiler_params=_cparams(("arbitrary", "arbitrary")),
        name="moe_up",
    )(te, nv, xs, wg, wu)


def _moe_down_body(te_ref, nv_ref, hid_ref, wd_ref, o_ref, wdb_s):
    i = pl.program_id(1)
    valid = i < nv_ref[0]
    changed = (i == 0) | (te_ref[i] != te_ref[jnp.maximum(i - 1, 0)])

    @pl.when(valid & changed)
    def _():
        wdb_s[...] = wd_ref[0].astype(BF16)

    @pl.when(valid)
    def _():
        o_ref[...] = _dot(hid_ref[...], wdb_s[...])

    @pl.when(jnp.logical_not(valid))
    def _():
        o_ref[...] = jnp.zeros_like(o_ref)


def _moe_down(hid, wd, te, nv, tm, nc=1024):
    n_slots, ff = hid.shape
    d = wd.shape[2]
    max_tiles = n_slots // tm
    grid_spec = pltpu.PrefetchScalarGridSpec(
        num_scalar_prefetch=2,
        grid=(d // nc, max_tiles),
        in_specs=[pl.BlockSpec((tm, ff), lambda j, i, te_r, nv_r: (jnp.minimum(i, nv_r[0] - 1), 0)),
                  pl.BlockSpec((1, ff, nc), lambda j, i, te_r, nv_r: (te_r[i], 0, j))],
        out_specs=pl.BlockSpec((tm, nc), lambda j, i, te_r, nv_r: (i, j)),
        scratch_shapes=[pltpu.VMEM((ff, nc), BF16)],
    )
    return pl.pallas_call(
        _moe_down_body,
        grid_spec=grid_spec,
        out_shape=jax.ShapeDtypeStruct((n_slots, d), F32),
        compiler_params=_cparams(("arbitrary", "arbitrary")),
        name="moe_down",
    )(te, nv, hid, wd)


def _combine_body(dest_ref, x_ref, gate_ref, mod_ref, ys_ref, *rest, tc, final):
    if final:
        lnf_ref, o_ref, buf_s, sem = rest
    else:
        o_ref, buf_s, sem = rest

    def row_in(k, t, d):
        return pltpu.make_async_copy(ys_ref.at[pl.ds(d, 1), :], buf_s.at[k, pl.ds(t, 1), :], sem)

    def issue(t, carry):
        row_in(0, t, dest_ref[0, 0, t]).start()
        row_in(1, t, dest_ref[0, 0, tc + t]).start()
        return carry

    lax.fori_loop(0, tc, issue, 0)

    def drain(t, carry):
        row_in(0, 0, 0).wait()
        row_in(1, 0, 0).wait()
        return carry

    lax.fori_loop(0, tc, drain, 0)

    gates = gate_ref[...]
    y = x_ref[...] + mod_ref[0, 5:6, :] * (gates[:, 0:1] * buf_s[0] + gates[:, 1:2] * buf_s[1])
    if final:
        y = _rms(y) * lnf_ref[...]
    o_ref[...] = y


def _combine(x, ys, dest, gates, mod, seq, ln_final=None, tc=256):
    n, d = x.shape
    tpb = seq // tc
    final = ln_final is not None
    dest_blk = jnp.concatenate([dest[0].reshape(n // tc, 1, tc), dest[1].reshape(n // tc, 1, tc)], axis=2)
    in_specs = [pl.BlockSpec((1, 1, 2 * tc), lambda i: (i, 0, 0), memory_space=pltpu.SMEM),
                pl.BlockSpec((tc, d), lambda i: (i, 0)),
                pl.BlockSpec((tc, 2), lambda i: (i, 0)),
                pl.BlockSpec((1, 6, d), lambda i: (i // tpb, 0, 0)),
                pl.BlockSpec(memory_space=pl.ANY)]
    args = [dest_blk, x, gates.T, mod, ys]
    if final:
        in_specs.append(pl.BlockSpec((1, d), lambda i: (0, 0)))
        args.append(ln_final.reshape(1, d))
    return pl.pallas_call(
        functools.partial(_combine_body, tc=tc, final=final),
        grid=(n // tc,),
        in_specs=in_specs,
        out_specs=pl.BlockSpec((tc, d), lambda i: (i, 0)),
        out_shape=jax.ShapeDtypeStruct((n, d), F32),
        scratch_shapes=[pltpu.VMEM((2, tc, d), F32), pltpu.SemaphoreType.DMA(())],
        compiler_params=_cparams(("arbitrary",)),
        name="moe_combine",
    )(*args)


def _moe(x, ln_g, mod, router_w, router_b, wg, wu, wd, seq, ln_final=None, tm=512):
    n, d = x.shape
    h, eidx, gates, rank, cnt = _route(x, ln_g, mod, router_w, router_b, seq)
    counts = cnt[:, 0]
    ntile = (counts + tm - 1) // tm
    cum = jnp.cumsum(ntile)
    off = (cum - ntile) * tm
    max_tiles = (2 * n) // tm + N_EXPERTS
    nv = cum[-1:].astype(I32)
    tiles = jnp.minimum(jnp.arange(max_tiles, dtype=I32), nv[0] - 1)
    te = jnp.minimum(jnp.sum((tiles[:, None] >= cum[None, :]).astype(I32), axis=1), N_EXPERTS - 1)
    expert_ids = jnp.arange(N_EXPERTS, dtype=I32)[:, None, None]
    dest = rank + jnp.sum(jnp.where(eidx[None] == expert_ids, off[:, None, None], 0), axis=0)
    dest = dest.astype(I32)
    meta = jnp.concatenate([cum * tm, ntile, nv]).astype(I32)
    xs = _dispatch(h, dest, meta, max_tiles * tm, tm)
    hid = _moe_up(xs, wg, wu, te, nv, tm)
    ys = _moe_down(hid, wd, te, nv, tm)
    return _combine(x, ys, dest, gates, mod, seq, ln_final)


def kernel(x, c, ln_attn_0, ada_w_0, ada_b_0, w_in_0, hgrn_norm_0, na_rpb_0, w_out_0, ln_ffn_0, moe_wg_0, moe_wu_0, moe_wd_0, ln_attn_1, ada_w_1, ada_b_1, w_in_1, diff_lambda_1, diff_norm_1, w_out_1, ln_ffn_1, moe_wg_1, moe_wu_1, moe_wd_1, hgrn_lb_logits, router_w, router_b, ln_final):
    bsz, seq, d = x.shape
    n = bsz * seq
    xf = x.reshape(n, d)
    mod0 = _ada(c, ada_w_0, ada_b_0)
    mod1 = _ada(c, ada_w_1, ada_b_1)

    h = _norm(xf, ln_attn_0, mod0, 0, 1, seq)
    proj = _mm(h, w_in_0.astype(BF16)).reshape(bsz, seq, -1)
    o_a = _hgrn(proj, hgrn_lb_logits, hgrn_norm_0)
    o_b = _na(proj, na_rpb_0, 5 * HG_HEADS * HG_D)
    xf = _mm_res([o_a.reshape(n, -1), o_b.reshape(n, -1)], w_out_0.astype(BF16), xf, mod0, 2, seq)
    xf = _moe(xf, ln_ffn_0, mod0, router_w, router_b, moe_wg_0, moe_wu_0, moe_wd_0, seq)

    h = _norm(xf, ln_attn_1, mod1, 0, 1, seq)
    proj = _mm_rope(h, w_in_1.astype(BF16), seq, 4 * DA_HEADS * DA_DH).reshape(bsz, seq, -1)
    o_c = _da(proj, diff_lambda_1, diff_norm_1, 1)
    xf = _mm_res([o_c.reshape(n, -1)], w_out_1.astype(BF16), xf, mod1, 2, seq)
    xf = _moe(xf, ln_ffn_1, mod1, router_w, router_b, moe_wg_1, moe_wu_1, moe_wd_1, seq, ln_final=ln_final)
    return xf.reshape(bsz, seq, d)
```

```python
import functools
import math

import jax
import jax.numpy as jnp
from jax import lax
from jax.experimental import pallas as pl
from jax.experimental.pallas import tpu as pltpu

F32 = jnp.float32
BF16 = jnp.bfloat16
I32 = jnp.int32

EPS = 1e-6
LANES = 128
GRID_W = 64
HG_HEADS = 8
HG_D = 128
HG_CHUNK = 32
HG_BLK = 256
HG_SBLK = 128
NA_HEADS = 16
NA_DH = 64
NA_KH = 8
NA_KW = 16
NA_ROWS_PER_STEP = 4
DA_HEADS = 8
DA_DH = 128
DA_ROW_PIECES = 2
ROPE_THETA = 500000.0
ROPE_DIM = DA_DH // 4
N_EXPERTS = 16
N_GROUPS = 4
EXPERTS_PER_GROUP = 4
NEG_BIG = -1e30
ZERO_ROWS = 64

VMEM_LIMIT = 56 * 1024 * 1024


def _cparams(sem):
    return pltpu.CompilerParams(dimension_semantics=sem, vmem_limit_bytes=VMEM_LIMIT)


def _dot(a, b):
    return jnp.dot(a, b, preferred_element_type=F32)


def _dot_nt(a, b):
    return lax.dot_general(a, b, (((1,), (1,)), ((), ())), preferred_element_type=F32)


def _silu(x):
    return x * jax.nn.sigmoid(x)


def _rms(x):
    return x * lax.rsqrt(jnp.mean(x * x, axis=-1, keepdims=True) + EPS)


def _ada_body(c_ref, w_ref, b_ref, o_ref):
    a = _silu(c_ref[...]).astype(BF16)
    o_ref[...] = _dot(a, w_ref[...].astype(BF16)) + b_ref[...]


def _ada(c, w, b):
    bsz, d = c.shape
    n = w.shape[1]
    rows = 16
    c16 = jnp.concatenate([c, jnp.zeros((rows - bsz, d), c.dtype)], axis=0)
    tn = 1024
    out = pl.pallas_call(
        _ada_body,
        grid=(n // tn,),
        in_specs=[pl.BlockSpec((rows, d), lambda j: (0, 0)),
                  pl.BlockSpec((d, tn), lambda j: (0, j)),
                  pl.BlockSpec((1, tn), lambda j: (0, j))],
        out_specs=pl.BlockSpec((rows, tn), lambda j: (0, j)),
        out_shape=jax.ShapeDtypeStruct((rows, n), F32),
        compiler_params=_cparams(("arbitrary",)),
        name="ada_mod",
    )(c16, w, b.reshape(1, n))
    return out[:bsz].reshape(bsz, 6, d)


def _norm_mod(x, g, sh, sc):
    return (_rms(x) * g) * (1.0 + sc) + sh


def _norm_body(x_ref, g_ref, mod_ref, o_ref, *, sh_idx, sc_idx):
    h = _norm_mod(x_ref[...], g_ref[...], mod_ref[0, sh_idx:sh_idx + 1, :], mod_ref[0, sc_idx:sc_idx + 1, :])
    o_ref[...] = h.astype(o_ref.dtype)


def _norm(x, g, mod, sh_idx, sc_idx, seq, tm=512):
    n, d = x.shape
    tpb = seq // tm
    return pl.pallas_call(
        functools.partial(_norm_body, sh_idx=sh_idx, sc_idx=sc_idx),
        grid=(n // tm,),
        in_specs=[pl.BlockSpec((tm, d), lambda i: (i, 0)),
                  pl.BlockSpec((1, d), lambda i: (0, 0)),
                  pl.BlockSpec((1, 6, d), lambda i: (i // tpb, 0, 0))],
        out_specs=pl.BlockSpec((tm, d), lambda i: (i, 0)),
        out_shape=jax.ShapeDtypeStruct((n, d), BF16),
        compiler_params=_cparams(("arbitrary",)),
        name="norm_mod",
    )(x, g.reshape(1, d), mod)


def _mm_body(a_ref, w_ref, o_ref):
    o_ref[...] = _dot(a_ref[...], w_ref[...]).astype(o_ref.dtype)


def _mm(a, w, tm=1024, tn=1024):
    n, k = a.shape
    nout = w.shape[1]
    tm = min(tm, n)
    return pl.pallas_call(
        _mm_body,
        grid=(n // tm, nout // tn),
        in_specs=[pl.BlockSpec((tm, k), lambda i, j: (i, 0)),
                  pl.BlockSpec((k, tn), lambda i, j: (0, j))],
        out_specs=pl.BlockSpec((tm, tn), lambda i, j: (i, j)),
        out_shape=jax.ShapeDtypeStruct((n, nout), BF16),
        compiler_params=_cparams(("arbitrary", "arbitrary")),
        name="proj_in",
    )(a, w)


def _mm_rope_body(a_ref, w_ref, cos_ref, sa_ref, sb_ref, o_ref, *, n_rope_tiles, tn):
    acc = _dot(a_ref[...], w_ref[...])
    j = pl.program_id(1)

    @pl.when(j < n_rope_tiles)
    def _():
        cos, sa, sb = cos_ref[...], sa_ref[...], sb_ref[...]
        half = ROPE_DIM // 2
        for c in range(tn // LANES):
            seg = acc[:, c * LANES:(c + 1) * LANES]
            r = seg * cos + pltpu.roll(seg, LANES - half, 1) * sa + pltpu.roll(seg, half, 1) * sb
            o_ref[:, c * LANES:(c + 1) * LANES] = r.astype(o_ref.dtype)

    @pl.when(j >= n_rope_tiles)
    def _():
        o_ref[...] = acc.astype(o_ref.dtype)


def _rope_tables(seq):
    half = ROPE_DIM // 2
    inv = ROPE_THETA ** (-jnp.arange(half, dtype=F32) / half)
    ang = jnp.arange(seq, dtype=F32)[:, None] * inv[None, :]
    cos, sin = jnp.cos(ang), jnp.sin(ang)
    pad = jnp.zeros((seq, LANES - ROPE_DIM), F32)
    z = jnp.zeros((seq, half), F32)
    cos_t = jnp.concatenate([cos, cos, pad + 1.0], axis=1)
    sa_t = jnp.concatenate([-sin, z, pad], axis=1)
    sb_t = jnp.concatenate([z, sin, pad], axis=1)
    return cos_t, sa_t, sb_t


def _mm_rope(a, w, seq, rope_cols, tm=1024, tn=1024):
    n, k = a.shape
    nout = w.shape[1]
    tm = min(tm, seq)
    tpb = seq // tm
    cos_t, sa_t, sb_t = _rope_tables(seq)
    tab_spec = pl.BlockSpec((tm, LANES), lambda i, j: (i % tpb, 0))
    return pl.pallas_call(
        functools.partial(_mm_rope_body, n_rope_tiles=rope_cols // tn, tn=tn),
        grid=(n // tm, nout // tn),
        in_specs=[pl.BlockSpec((tm, k), lambda i, j: (i, 0)),
                  pl.BlockSpec((k, tn), lambda i, j: (0, j)),
                  tab_spec, tab_spec, tab_spec],
        out_specs=pl.BlockSpec((tm, tn), lambda i, j: (i, j)),
        out_shape=jax.ShapeDtypeStruct((n, nout), BF16),
        compiler_params=_cparams(("arbitrary", "arbitrary")),
        name="proj_in_rope",
    )(a, w, cos_t, sa_t, sb_t)


def _mm_res_body(*refs, n_a, gt_idx):
    a_refs = refs[:n_a]
    w_ref, x_ref, mod_ref, o_ref = refs[n_a:]
    acc = None
    k0 = 0
    for a_ref in a_refs:
        kk = a_ref.shape[1]
        part = _dot(a_ref[...], w_ref[k0:k0 + kk, :])
        acc = part if acc is None else acc + part
        k0 += kk
    o_ref[...] = x_ref[...] + mod_ref[0, gt_idx:gt_idx + 1, :] * acc


def _mm_res(a_list, w, x, mod, gt_idx, seq, tm=512, tn=2048):
    n, d = x.shape
    tpb = seq // tm
    in_specs = [pl.BlockSpec((tm, a.shape[1]), lambda i, j: (i, 0)) for a in a_list]
    in_specs += [pl.BlockSpec((w.shape[0], tn), lambda i, j: (0, j)),
                 pl.BlockSpec((tm, tn), lambda i, j: (i, j)),
                 pl.BlockSpec((1, 6, tn), lambda i, j: (i // tpb, 0, j))]
    return pl.pallas_call(
        functools.partial(_mm_res_body, n_a=len(a_list), gt_idx=gt_idx),
        grid=(n // tm, d // tn),
        in_specs=in_specs,
        out_specs=pl.BlockSpec((tm, tn), lambda i, j: (i, j)),
        out_shape=jax.ShapeDtypeStruct((n, d), F32),
        compiler_params=_cparams(("arbitrary", "arbitrary")),
        name="proj_out_res",
    )(*a_list, w, x, mod)


def _chunk_prefix(g):
    r = g.shape[0]
    sub_rows = 8
    per_chunk = HG_CHUNK // sub_rows
    x = g.reshape(r // sub_rows, sub_rows, HG_D)
    sub = lax.broadcasted_iota(I32, x.shape, 1)
    for s in (1, 2, 4):
        x = x + jnp.where(sub >= s, pltpu.roll(x, s, 1), 0.0)
    tot = x[:, sub_rows - 1:sub_rows, :].reshape(r // HG_CHUNK, per_chunk, 1, HG_D)
    c1 = tot[:, 0:1]
    c2 = c1 + tot[:, 1:2]
    c3 = c2 + tot[:, 2:3]
    carry = jnp.concatenate([jnp.zeros_like(c1), c1, c2, c3], axis=1)
    return x.reshape(r // HG_CHUNK, per_chunk, sub_rows, HG_D) + carry, c3 + tot[:, 3:4]


def _hgrn_body(q_ref, i_ref, ff_ref, fb_ref, gg_ref, lbl_ref, ng_ref, o_ref,
               qg_s, kd_s, dec_s, vt_s, oacc_s, *, seq):
    blk = HG_BLK
    sblk = HG_SBLK
    cpb = blk // HG_CHUNK
    cps = sblk // HG_CHUNK
    lg = lbl_ref[...]
    ex = jnp.exp(lg - jnp.max(lg, axis=0, keepdims=True))
    lb = ex[0:1, :] / jnp.sum(ex, axis=0, keepdims=True)

    row = lax.broadcasted_iota(I32, (blk, blk), 0)
    col = lax.broadcasted_iota(I32, (blk, blk), 1)
    same_chunk = (row // HG_CHUNK) == (col // HG_CHUNK)
    masks = (same_chunk & (row >= col), same_chunk & (row <= col))
    srow = lax.broadcasted_iota(I32, (sblk, HG_D), 0) // HG_CHUNK
    f_refs = (ff_ref, fb_ref)
    shape4 = (cpb, HG_CHUNK // 8, 8, HG_D)

    def stage1(b, carry):
        sl = pl.ds(pl.multiple_of(b * blk, blk), blk)
        qh = _silu(q_ref[0, sl, :].astype(F32)).reshape(shape4)
        v = i_ref[0, sl, :]
        vf = v.astype(F32)
        for half in range(blk // sblk):
            vt_s[b * (blk // sblk) + half] = vf[half * sblk:(half + 1) * sblk, :].T.astype(BF16)
        o_blk = None
        for d in range(2):
            f = lb + (1.0 - lb) * jax.nn.sigmoid(f_refs[d][0, sl, :].astype(F32))
            g = jnp.log(f)
            k = (1.0 - f).reshape(shape4)
            pref, total = _chunk_prefix(g)
            gc = pref if d == 0 else total - pref + g.reshape(shape4)
            eg = jnp.exp(gc)
            dec = jnp.exp(total)
            qg = (qh * eg).reshape(blk, HG_D).astype(BF16)
            kgf = k * (1.0 / eg)
            kg = kgf.reshape(blk, HG_D).astype(BF16)
            qg_s[d, sl, :] = qg
            kd_s[d, sl, :] = (kgf * dec).reshape(blk, HG_D).astype(BF16)
            dec_s[d, pl.ds(pl.multiple_of(b * cpb, cpb), cpb)] = dec.reshape(cpb, 1, HG_D)
            a = jnp.where(masks[d], _dot_nt(qg, kg), 0.0).astype(BF16)
            o = _dot(a, v)
            o_blk = o if o_blk is None else o_blk + o
        oacc_s[sl, :] = o_blk
        return carry

    lax.fori_loop(0, seq // blk, stage1, 0)

    n_sblk = seq // sblk

    def stage2(j, carry):
        states = list(carry)
        sbs = (j, n_sblk - 1 - j)
        uts = []
        for d in range(2):
            vt_b = vt_s[sbs[d]]
            kd_b = kd_s[d, pl.ds(pl.multiple_of(sbs[d] * sblk, sblk), sblk), :]
            uts.append([_dot(vt_b, jnp.where(srow == jj, kd_b, jnp.zeros_like(kd_b))) for jj in range(cps)])
        inter = [[None] * cps, [None] * cps]
        for step in range(cps):
            for d in range(2):
                jj = step if d == 0 else cps - 1 - step
                rows = pl.ds(pl.multiple_of(sbs[d] * sblk + jj * HG_CHUNK, HG_CHUNK), HG_CHUNK)
                inter[d][jj] = _dot_nt(qg_s[d, rows, :], states[d].astype(BF16))
                states[d] = states[d] * dec_s[d, sbs[d] * cps + jj] + uts[d][jj]
        for d in range(2):
            rows = pl.ds(pl.multiple_of(sbs[d] * sblk, sblk), sblk)
            oacc_s[rows, :] += jnp.concatenate(inter[d], axis=0)
        return tuple(states)

    zero = jnp.zeros((HG_D, HG_D), F32)
    lax.fori_loop(0, n_sblk, stage2, (zero, zero), unroll=2)

    y = _rms(oacc_s[...]) * ng_ref[0]
    o_ref[0] = (y * _silu(gg_ref[0].astype(F32))).astype(o_ref.dtype)


def _hgrn(proj, lb_logits, norm_g):
    bsz, seq, _ = proj.shape
    n_chunks = seq // HG_CHUNK
    n_layers = lb_logits.shape[0]

    def seg_spec(seg):
        return pl.BlockSpec((1, seq, HG_D), lambda b, h: (b, 0, seg * HG_HEADS + h))

    return pl.pallas_call(
        functools.partial(_hgrn_body, seq=seq),
        grid=(bsz, HG_HEADS),
        in_specs=[seg_spec(0), seg_spec(1), seg_spec(2), seg_spec(3), seg_spec(4),
                  pl.BlockSpec((n_layers, HG_D), lambda b, h: (0, h)),
                  pl.BlockSpec((1, 1, HG_D), lambda b, h: (h, 0, 0))],
        out_specs=pl.BlockSpec((1, seq, HG_D), lambda b, h: (b, 0, h)),
        out_shape=jax.ShapeDtypeStruct((bsz, seq, HG_HEADS * HG_D), BF16),
        scratch_shapes=[pltpu.VMEM((2, seq, HG_D), BF16), pltpu.VMEM((2, seq, HG_D), BF16),
                        pltpu.VMEM((2, n_chunks, 1, HG_D), F32),
                        pltpu.VMEM((seq // HG_SBLK, HG_D, HG_SBLK), BF16),
                        pltpu.VMEM((seq, HG_D), F32)],
        compiler_params=_cparams(("arbitrary", "arbitrary")),
        name="hgrn2",
    )(proj, proj, proj, proj, proj, lb_logits, norm_g.reshape(HG_HEADS, 1, HG_D))


def _na_bias_table(rpb):
    col = jnp.arange(GRID_W)
    dc = jnp.clip(col[None, :] - col[:, None] + (NA_KW - 1), 0, 2 * NA_KW - 2)
    sel = (dc[None] == jnp.arange(2 * NA_KW - 1)[:, None, None]).astype(F32)
    toe = jnp.einsum("hrd,dqk->hrqk", rpb.astype(F32), sel, precision=lax.Precision.HIGHEST)
    pats = []
    for p in range(NA_KH):
        w = toe[:, NA_KH - 1 - p:2 * NA_KH - 1 - p]
        pats.append(w.transpose(0, 2, 1, 3).reshape(rpb.shape[0], GRID_W, NA_KH * GRID_W))
    return jnp.stack(pats, axis=1)


def _na_body(q_ref, k_ref, v_ref, bt_ref, o_ref, *, seq):
    rows = seq // GRID_W
    win = NA_KH * GRID_W
    cq = lax.broadcasted_iota(I32, (GRID_W, win), 0)
    ck = lax.broadcasted_iota(I32, (GRID_W, win), 1) % GRID_W
    cs = jnp.clip(cq - NA_KW // 2, 0, GRID_W - NA_KW)
    in_win = (ck >= cs) & (ck < cs + NA_KW)
    scale = NA_DH ** -0.5
    heads_here = LANES // NA_DH
    lane_head = lax.broadcasted_iota(I32, (GRID_W, LANES), 1) // NA_DH
    out_head = lane_head

    def row_group(rg, carry):
        items = []
        for u in range(NA_ROWS_PER_STEP):
            r = rg * NA_ROWS_PER_STEP + u
            start = jnp.clip(r - NA_KH // 2, 0, rows - NA_KH)
            qs = pl.ds(pl.multiple_of(r * GRID_W, GRID_W), GRID_W)
            ks = pl.ds(pl.multiple_of(start * GRID_W, GRID_W), win)
            q = q_ref[0, qs, :]
            k = k_ref[0, ks, :]
            for hh in range(heads_here):
                qh = jnp.where(lane_head == hh, q, jnp.zeros_like(q))
                items.append((_dot_nt(qh, k), hh, r - start, qs, ks))
        probs = []
        for s, hh, pat, qs, ks in items:
            s = jnp.where(in_win, s * scale + bt_ref[hh, pat], NEG_BIG)
            e = jnp.exp(s - jnp.max(s, axis=-1, keepdims=True))
            probs.append((e.astype(BF16), 1.0 / jnp.sum(e, axis=-1, keepdims=True), qs, ks))
        for i in range(0, len(probs), heads_here):
            qs, ks = probs[i][2], probs[i][3]
            v = v_ref[0, ks, :]
            o = None
            for hh in range(heads_here):
                oh = _dot(probs[i + hh][0], v) * probs[i + hh][1]
                o = oh if o is None else jnp.where(out_head == hh, oh, o)
            o_ref[0, qs, :] = o.astype(o_ref.dtype)
        return carry

    lax.fori_loop(0, rows // NA_ROWS_PER_STEP, row_group, 0)


def _na(proj, rpb, col0):
    bsz, seq, _ = proj.shape
    hpb = LANES // NA_DH
    nblk = NA_HEADS // hpb
    b0 = col0 // LANES
    bt = _na_bias_table(rpb)

    def seg_spec(seg):
        return pl.BlockSpec((1, seq, LANES), lambda b, h: (b, 0, b0 + seg * nblk + h))

    return pl.pallas_call(
        functools.partial(_na_body, seq=seq),
        grid=(bsz, nblk),
        in_specs=[seg_spec(0), seg_spec(1), seg_spec(2),
                  pl.BlockSpec((hpb, NA_KH, GRID_W, NA_KH * GRID_W), lambda b, h: (h, 0, 0, 0))],
        out_specs=pl.BlockSpec((1, seq, LANES), lambda b, h: (b, 0, h)),
        out_shape=jax.ShapeDtypeStruct((bsz, seq, NA_HEADS * NA_DH), BF16),
        compiler_params=_cparams(("arbitrary", "arbitrary")),
        name="natten",
    )(proj, proj, proj, bt)


def _da_body(q_ref, k_ref, v_ref, lam_ref, g_ref, o_ref, *, lam_init):
    lp = lam_ref[...]
    lam = (jnp.exp(jnp.sum(lp[0:1] * lp[1:2], axis=-1, keepdims=True))
           - jnp.exp(jnp.sum(lp[2:3] * lp[3:4], axis=-1, keepdims=True)) + lam_init)
    scale2 = (DA_DH ** -0.5) * math.log2(math.e)
    v = v_ref[0]

    def weighted(s):
        e = jnp.exp2(s * scale2 - jnp.max(s, axis=-1, keepdims=True) * scale2)
        return _dot(e.astype(BF16), v), 1.0 / jnp.sum(e, axis=-1, keepdims=True)

    rows_per = q_ref.shape[1] // DA_ROW_PIECES
    pieces = [slice(p * rows_per, (p + 1) * rows_per) for p in range(DA_ROW_PIECES)]
    ss = [[_dot_nt(q_ref[0, rs, lo:lo + DA_DH], k_ref[0, :, lo:lo + DA_DH]) for lo in (0, DA_DH)]
          for rs in pieces]
    for rs, (s1, s2) in zip(pieces, ss):
        o1, r1 = weighted(s1)
        o2, r2 = weighted(s2)
        o = o1 * r1 - o2 * (lam * r2)
        o_ref[0, rs, :] = (_rms(o) * g_ref[...] * (1.0 - lam_init)).astype(o_ref.dtype)


def _da(proj, diff_lambda, diff_norm, layer_idx, tq=512):
    bsz, seq, _ = proj.shape
    tq = min(tq, seq)
    dv = 2 * DA_DH
    lam_init = 0.8 - 0.6 * math.exp(-0.3 * layer_idx)
    return pl.pallas_call(
        functools.partial(_da_body, lam_init=lam_init),
        grid=(bsz, DA_HEADS, seq // tq),
        in_specs=[pl.BlockSpec((1, tq, dv), lambda b, h, i: (b, i, h)),
                  pl.BlockSpec((1, seq, dv), lambda b, h, i: (b, 0, DA_HEADS + h)),
                  pl.BlockSpec((1, seq, dv), lambda b, h, i: (b, 0, 2 * DA_HEADS + h)),
                  pl.BlockSpec((4, DA_DH), lambda b, h, i: (0, 0)),
                  pl.BlockSpec((1, dv), lambda b, h, i: (0, 0))],
        out_specs=pl.BlockSpec((1, tq, dv), lambda b, h, i: (b, i, h)),
        out_shape=jax.ShapeDtypeStruct((bsz, seq, DA_HEADS * dv), BF16),
        compiler_params=_cparams(("arbitrary", "arbitrary", "arbitrary")),
        name="diff_attn",
    )(proj, proj, proj, diff_lambda, diff_norm.reshape(1, dv))


def _route_body(x_ref, g_ref, mod_ref, rwc_ref, rwh_ref, rb_ref,
                h_ref, e_ref, gate_ref, rank_ref, cnt_ref, run_s, *, tm):
    i = pl.program_id(0)

    @pl.when(i == 0)
    def _():
        run_s[...] = jnp.zeros_like(run_s)

    h = _norm_mod(x_ref[...], g_ref[...], mod_ref[0, 3:4, :], mod_ref[0, 4:5, :])
    h_ref[...] = h
    h_hi = h.astype(BF16)
    h_lo = (h - h_hi.astype(F32)).astype(BF16)
    both = _dot_nt(rwc_ref[...], h_hi)
    logits = both[0:N_EXPERTS] + both[N_EXPERTS:2 * N_EXPERTS] + _dot_nt(rwh_ref[...], h_lo)
    scores = jax.nn.sigmoid(logits)
    sel = scores + rb_ref[...]

    sc = [scores[e:e + 1, :] for e in range(N_EXPERTS)]
    sl = [sel[e:e + 1, :] for e in range(N_EXPERTS)]

    def top2_sum(a, b, c, d):
        hi1, lo1 = jnp.maximum(a, b), jnp.minimum(a, b)
        hi2, lo2 = jnp.maximum(c, d), jnp.minimum(c, d)
        return jnp.maximum(hi1, hi2) + jnp.maximum(jnp.minimum(hi1, hi2), jnp.maximum(lo1, lo2))

    gs = [top2_sum(*sl[4 * g:4 * g + 4]) for g in range(N_GROUPS)]
    best = jnp.zeros_like(gs[0], dtype=I32)
    best_v = gs[0]
    for g in range(1, N_GROUPS):
        better = gs[g] > best_v
        best = jnp.where(better, g, best)
        best_v = jnp.where(better, gs[g], best_v)

    def pick(vals, j):
        out = vals[j]
        for g in range(1, N_GROUPS):
            out = jnp.where(best == g, vals[4 * g + j], out)
        return out

    v = [pick(sl, j) for j in range(EXPERTS_PER_GROUP)]
    w = [pick(sc, j) for j in range(EXPERTS_PER_GROUP)]

    def argmax4(vals, excluded):
        idx = jnp.full_like(best, -1)
        cur = jnp.full_like(vals[0], -jnp.inf)
        for j in range(EXPERTS_PER_GROUP):
            ok = (vals[j] > cur) if excluded is None else ((vals[j] > cur) & (excluded != j))
            idx = jnp.where(ok, j, idx)
            cur = jnp.where(ok, vals[j], cur)
        return idx

    i1 = argmax4(v, None)
    i2 = argmax4(v, i1)

    def take(vals, idx):
        out = vals[0]
        for j in range(1, EXPERTS_PER_GROUP):
            out = jnp.where(idx == j, vals[j], out)
        return out

    w1, w2 = take(w, i1), take(w, i2)
    den = w1 + w2
    e1 = best * EXPERTS_PER_GROUP + i1
    e2 = best * EXPERTS_PER_GROUP + i2
    e_ref[0:1, :] = e1
    e_ref[1:2, :] = e2
    gate_ref[0:1, :] = w1 / den
    gate_ref[1:2, :] = w2 / den

    eid = lax.broadcasted_iota(I32, (N_EXPERTS, tm), 0)
    oh1 = eid == e1
    oh2 = eid == e2
    onehot = (oh1 | oh2).astype(BF16)
    srow = lax.broadcasted_iota(I32, (tm, tm), 0)
    tcol = lax.broadcasted_iota(I32, (tm, tm), 1)
    before = (srow < tcol).astype(BF16)
    pos = _dot(onehot, before) + run_s[...]
    rank_ref[0:1, :] = jnp.sum(jnp.where(oh1, pos, 0.0), axis=0, keepdims=True).astype(I32)
    rank_ref[1:2, :] = jnp.sum(jnp.where(oh2, pos, 0.0), axis=0, keepdims=True).astype(I32)
    run_s[...] = run_s[...] + jnp.sum(onehot.astype(F32), axis=1, keepdims=True)
    cnt_ref[...] = jnp.broadcast_to(run_s[...], cnt_ref.shape).astype(I32)


def _route(x, g, mod, router_w, router_b, seq, tm=512):
    n, d = x.shape
    tpb = seq // tm
    rwt = router_w.T
    rw_hi = rwt.astype(BF16)
    rw_lo = (rwt - rw_hi.astype(F32)).astype(BF16)
    rwc = jnp.concatenate([rw_hi, rw_lo], axis=0)
    pair_spec = pl.BlockSpec((2, tm), lambda i: (0, i))
    return pl.pallas_call(
        functools.partial(_route_body, tm=tm),
        grid=(n // tm,),
        in_specs=[pl.BlockSpec((tm, d), lambda i: (i, 0)),
                  pl.BlockSpec((1, d), lambda i: (0, 0)),
                  pl.BlockSpec((1, 6, d), lambda i: (i // tpb, 0, 0)),
                  pl.BlockSpec((2 * N_EXPERTS, d), lambda i: (0, 0)),
                  pl.BlockSpec((N_EXPERTS, d), lambda i: (0, 0)),
                  pl.BlockSpec((N_EXPERTS, 1), lambda i: (0, 0))],
        out_specs=[pl.BlockSpec((tm, d), lambda i: (i, 0)), pair_spec, pair_spec, pair_spec,
                   pl.BlockSpec((N_EXPERTS, LANES), lambda i: (0, 0))],
        out_shape=[jax.ShapeDtypeStruct((n, d), F32),
                   jax.ShapeDtypeStruct((2, n), I32),
                   jax.ShapeDtypeStruct((2, n), F32),
                   jax.ShapeDtypeStruct((2, n), I32),
                   jax.ShapeDtypeStruct((N_EXPERTS, LANES), I32)],
        scratch_shapes=[pltpu.VMEM((N_EXPERTS, 1), F32)],
        compiler_params=_cparams(("arbitrary",)),
        name="moe_route",
    )(x, g.reshape(1, d), mod, rwc, rw_hi, router_b.reshape(N_EXPERTS, 1).astype(F32))


def _row_copy(src_ref, s, dst_ref, d, sem):
    return pltpu.make_async_copy(src_ref.at[pl.ds(s, 1), :], dst_ref.at[pl.ds(d, 1), :], sem)


def _dispatch_body(meta_ref, dest_ref, h_ref, xs_ref, zero_s, sem, *, tmd, tm, max_tiles):
    i = pl.program_id(0)
    zr = zero_s.shape[0]

    def zero_tile(start):
        copies = [pltpu.make_async_copy(zero_s, xs_ref.at[pl.ds(pl.multiple_of(start + z * zr, zr), zr), :], sem)
                  for z in range(tm // zr)]
        for cp in copies:
            cp.start()
        for cp in copies:
            cp.wait()

    @pl.when(i == 0)
    def _():
        zero_s[...] = jnp.zeros_like(zero_s)
        for e in range(N_EXPERTS):
            @pl.when(meta_ref[N_EXPERTS + e] > 0)
            def _():
                zero_tile(pl.multiple_of(meta_ref[e] - tm, tm))

        def fill_unused(t, carry):
            zero_tile(pl.multiple_of(t * tm, tm))
            return carry

        lax.fori_loop(meta_ref[2 * N_EXPERTS], max_tiles, fill_unused, 0)

    def issue(t, carry):
        _row_copy(h_ref, t, xs_ref, dest_ref[0, 0, t], sem).start(priority=0)
        _row_copy(h_ref, t, xs_ref, dest_ref[0, 0, tmd + t], sem).start(priority=1)
        return carry

    lax.fori_loop(0, tmd, issue, 0, unroll=4)

    def drain(t, carry):
        _row_copy(h_ref, 0, xs_ref, 0, sem).wait()
        _row_copy(h_ref, 0, xs_ref, 0, sem).wait()
        return carry

    lax.fori_loop(0, tmd, drain, 0)


def _dispatch(h, dest, meta, n_slots, tm, tmd=256):
    n, d = h.shape
    dest_blk = jnp.concatenate([dest[0].reshape(n // tmd, 1, tmd), dest[1].reshape(n // tmd, 1, tmd)], axis=2)
    grid_spec = pltpu.PrefetchScalarGridSpec(
        num_scalar_prefetch=1,
        grid=(n // tmd,),
        in_specs=[pl.BlockSpec((1, 1, 2 * tmd), lambda i, m: (i, 0, 0), memory_space=pltpu.SMEM),
                  pl.BlockSpec((tmd, d), lambda i, m: (i, 0))],
        out_specs=pl.BlockSpec(memory_space=pl.ANY),
        scratch_shapes=[pltpu.VMEM((ZERO_ROWS, d), F32), pltpu.SemaphoreType.DMA(())],
    )
    return pl.pallas_call(
        functools.partial(_dispatch_body, tmd=tmd, tm=tm, max_tiles=n_slots // tm),
        grid_spec=grid_spec,
        out_shape=jax.ShapeDtypeStruct((n_slots, d), F32),
        compiler_params=_cparams(("arbitrary",)),
        name="moe_dispatch",
    )(meta, dest_blk, h)


def _moe_up_body(te_ref, nv_ref, xs_ref, wg_ref, wu_ref, hid_ref, wgb_s, wub_s):
    i = pl.program_id(1)
    valid = i < nv_ref[0]
    changed = (i == 0) | (te_ref[i] != te_ref[jnp.maximum(i - 1, 0)])

    @pl.when(valid & changed)
    def _():
        wgb_s[...] = wg_ref[0].astype(BF16)
        wub_s[...] = wu_ref[0].astype(BF16)

    @pl.when(valid)
    def _():
        x = xs_ref[...].astype(BF16)
        a = _dot(x, wgb_s[...])
        u = _dot(x, wub_s[...])
        hid_ref[...] = (_silu(a) * u).astype(hid_ref.dtype)

    @pl.when(jnp.logical_not(valid))
    def _():
        hid_ref[...] = jnp.zeros_like(hid_ref)


def _moe_up(xs, wg, wu, te, nv, tm, fc=512):
    n_slots, d = xs.shape
    ff = wg.shape[2]
    max_tiles = n_slots // tm

    def row_map(j, i, te_r, nv_r):
        return (jnp.minimum(i, nv_r[0] - 1), 0)

    def w_map(j, i, te_r, nv_r):
        return (te_r[i], 0, j)

    grid_spec = pltpu.PrefetchScalarGridSpec(
        num_scalar_prefetch=2,
        grid=(ff // fc, max_tiles),
        in_specs=[pl.BlockSpec((tm, d), row_map),
                  pl.BlockSpec((1, d, fc), w_map),
                  pl.BlockSpec((1, d, fc), w_map)],
        out_specs=pl.BlockSpec((tm, fc), lambda j, i, te_r, nv_r: (i, j)),
        scratch_shapes=[pltpu.VMEM((d, fc), BF16), pltpu.VMEM((d, fc), BF16)],
    )
    return pl.pallas_call(
        _moe_up_body,
        grid_spec=grid_spec,
        out_shape=jax.ShapeDtypeStruct((n_slots, ff), BF16),
        compiler_params=_cparams(("arbitrary", "arbitrary")),
        name="moe_up",
    )(te, nv, xs, wg, wu)


def _moe_down_body(te_ref, nv_ref, hid_ref, wd_ref, o_ref, wdb_s):
    i = pl.program_id(1)
    valid = i < nv_ref[0]
    changed = (i == 0) | (te_ref[i] != te_ref[jnp.maximum(i - 1, 0)])

    @pl.when(valid & changed)
    def _():
        wdb_s[...] = wd_ref[0].astype(BF16)

    @pl.when(valid)
    def _():
        o_ref[...] = _dot(hid_ref[...], wdb_s[...])

    @pl.when(jnp.logical_not(valid))
    def _():
        o_ref[...] = jnp.zeros_like(o_ref)


def _moe_down(hid, wd, te, nv, tm, nc=2048):
    n_slots, ff = hid.shape
    d = wd.shape[2]
    max_tiles = n_slots // tm
    grid_spec = pltpu.PrefetchScalarGridSpec(
        num_scalar_prefetch=2,
        grid=(d // nc, max_tiles),
        in_specs=[pl.BlockSpec((tm, ff), lambda j, i, te_r, nv_r: (jnp.minimum(i, nv_r[0] - 1), 0)),
                  pl.BlockSpec((1, ff, nc), lambda j, i, te_r, nv_r: (te_r[i], 0, j))],
        out_specs=pl.BlockSpec((tm, nc), lambda j, i, te_r, nv_r: (i, j)),
        scratch_shapes=[pltpu.VMEM((ff, nc), BF16)],
    )
    return pl.pallas_call(
        _moe_down_body,
        grid_spec=grid_spec,
        out_shape=jax.ShapeDtypeStruct((n_slots, d), F32),
        compiler_params=_cparams(("arbitrary", "arbitrary")),
        name="moe_down",
    )(te, nv, hid, wd)


def _combine_body(dest_ref, x_ref, gate_ref, mod_ref, ys_ref, *rest, tc, final, with_next):
    hn_ref = None
    if final:
        lnf_ref, o_ref, buf_s, sem = rest
    elif with_next:
        lng_ref, modn_ref, o_ref, hn_ref, buf_s, sem = rest
    else:
        o_ref, buf_s, sem = rest

    def row_in(k, t, d):
        return pltpu.make_async_copy(ys_ref.at[pl.ds(d, 1), :], buf_s.at[k, pl.ds(t, 1), :], sem)

    def issue(t, carry):
        row_in(0, t, dest_ref[0, 0, t]).start(priority=0)
        row_in(1, t, dest_ref[0, 0, tc + t]).start(priority=1)
        return carry

    lax.fori_loop(0, tc, issue, 0, unroll=4)

    def drain(t, carry):
        row_in(0, 0, 0).wait()
        row_in(1, 0, 0).wait()
        return carry

    lax.fori_loop(0, tc, drain, 0)

    gates = gate_ref[...]
    y = x_ref[...] + mod_ref[0, 5:6, :] * (gates[:, 0:1] * buf_s[0] + gates[:, 1:2] * buf_s[1])
    if final:
        y = _rms(y) * lnf_ref[...]
    o_ref[...] = y
    if hn_ref is not None:
        hn_ref[...] = _norm_mod(y, lng_ref[...], modn_ref[0, 0:1, :], modn_ref[0, 1:2, :]).astype(hn_ref.dtype)


def _combine(x, ys, dest, gates, mod, seq, ln_final=None, next_norm=None, tc=256):
    n, d = x.shape
    tpb = seq // tc
    final = ln_final is not None
    with_next = next_norm is not None
    dest_blk = jnp.concatenate([dest[0].reshape(n // tc, 1, tc), dest[1].reshape(n // tc, 1, tc)], axis=2)
    in_specs = [pl.BlockSpec((1, 1, 2 * tc), lambda i: (i, 0, 0), memory_space=pltpu.SMEM),
                pl.BlockSpec((tc, d), lambda i: (i, 0)),
                pl.BlockSpec((tc, 2), lambda i: (i, 0)),
                pl.BlockSpec((1, 6, d), lambda i: (i // tpb, 0, 0)),
                pl.BlockSpec(memory_space=pl.ANY)]
    args = [dest_blk, x, gates.T, mod, ys]
    if final:
        in_specs.append(pl.BlockSpec((1, d), lambda i: (0, 0)))
        args.append(ln_final.reshape(1, d))
    row_spec = pl.BlockSpec((tc, d), lambda i: (i, 0))
    out_specs, out_shape = row_spec, jax.ShapeDtypeStruct((n, d), F32)
    if with_next:
        in_specs += [pl.BlockSpec((1, d), lambda i: (0, 0)), pl.BlockSpec((1, 6, d), lambda i: (i // tpb, 0, 0))]
        args += [next_norm[0].reshape(1, d), next_norm[1]]
        out_specs, out_shape = [row_spec, row_spec], [out_shape, jax.ShapeDtypeStruct((n, d), BF16)]
    return pl.pallas_call(
        functools.partial(_combine_body, tc=tc, final=final, with_next=with_next),
        grid=(n // tc,),
        in_specs=in_specs,
        out_specs=out_specs,
        out_shape=out_shape,
        scratch_shapes=[pltpu.VMEM((2, tc, d), F32), pltpu.SemaphoreType.DMA(())],
        compiler_params=_cparams(("arbitrary",)),
        name="moe_combine",
    )(*args)


def _moe(x, ln_g, mod, router_w, router_b, wg, wu, wd, seq, ln_final=None, next_norm=None, tm=512):
    n, d = x.shape
    h, eidx, gates, rank, cnt = _route(x, ln_g, mod, router_w, router_b, seq)
    counts = cnt[:, 0]
    ntile = (counts + tm - 1) // tm
    cum = jnp.cumsum(ntile)
    off = (cum - ntile) * tm
    max_tiles = (2 * n) // tm + N_EXPERTS
    nv = cum[-1:].astype(I32)
    tiles = jnp.minimum(jnp.arange(max_tiles, dtype=I32), nv[0] - 1)
    te = jnp.minimum(jnp.sum((tiles[:, None] >= cum[None, :]).astype(I32), axis=1), N_EXPERTS - 1)
    expert_ids = jnp.arange(N_EXPERTS, dtype=I32)[:, None, None]
    dest = rank + jnp.sum(jnp.where(eidx[None] == expert_ids, off[:, None, None], 0), axis=0)
    dest = dest.astype(I32)
    meta = jnp.concatenate([cum * tm, ntile, nv]).astype(I32)
    xs = _dispatch(h, dest, meta, max_tiles * tm, tm)
    hid = _moe_up(xs, wg, wu, te, nv, tm)
    ys = _moe_down(hid, wd, te, nv, tm)
    return _combine(x, ys, dest, gates, mod, seq, ln_final, next_norm)


def kernel(x, c, ln_attn_0, ada_w_0, ada_b_0, w_in_0, hgrn_norm_0, na_rpb_0, w_out_0, ln_ffn_0, moe_wg_0, moe_wu_0, moe_wd_0, ln_attn_1, ada_w_1, ada_b_1, w_in_1, diff_lambda_1, diff_norm_1, w_out_1, ln_ffn_1, moe_wg_1, moe_wu_1, moe_wd_1, hgrn_lb_logits, router_w, router_b, ln_final):
    bsz, seq, d = x.shape
    n = bsz * seq
    xf = x.reshape(n, d)
    mod0 = _ada(c, ada_w_0, ada_b_0)
    mod1 = _ada(c, ada_w_1, ada_b_1)

    h = _norm(xf, ln_attn_0, mod0, 0, 1, seq)
    proj = _mm(h, w_in_0.astype(BF16)).reshape(bsz, seq, -1)
    o_a = _hgrn(proj, hgrn_lb_logits, hgrn_norm_0)
    o_b = _na(proj, na_rpb_0, 5 * HG_HEADS * HG_D)
    xf = _mm_res([o_a.reshape(n, -1), o_b.reshape(n, -1)], w_out_0.astype(BF16), xf, mod0, 2, seq)
    xf, h = _moe(xf, ln_ffn_0, mod0, router_w, router_b, moe_wg_0, moe_wu_0, moe_wd_0, seq,
                 next_norm=(ln_attn_1, mod1))

    proj = _mm_rope(h, w_in_1.astype(BF16), seq, 4 * DA_HEADS * DA_DH).reshape(bsz, seq, -1)
    o_c = _da(proj, diff_lambda_1, diff_norm_1, 1)
    xf = _mm_res([o_c.reshape(n, -1)], w_out_1.astype(BF16), xf, mod1, 2, seq)
    xf = _moe(xf, ln_ffn_1, mod1, router_w, router_b, moe_wg_1, moe_wu_1, moe_wd_1, seq, ln_final=ln_final)
    return xf.reshape(bsz, seq, d)
```

```python
import functools
import math

import jax
import jax.numpy as jnp
from jax import lax
from jax.experimental import pallas as pl
from jax.experimental.pallas import tpu as pltpu

F32 = jnp.float32
BF16 = jnp.bfloat16
I32 = jnp.int32

EPS = 1e-6
LANES = 128
MXU_N = 256
GRID_W = 64
HG_HEADS = 8
HG_D = 128
HG_CHUNK = 32
HG_BLK = 256
HG_SBLK = 128
HG_HEADS_PER_STEP = 4
NA_HEADS = 16
NA_DH = 64
NA_KH = 8
NA_KW = 16
NA_ROWS_PER_STEP = 4
DA_HEADS = 8
DA_DH = 128
DA_ROW_PIECES = 2
ROPE_THETA = 500000.0
ROPE_DIM = DA_DH // 4
N_EXPERTS = 16
N_GROUPS = 4
EXPERTS_PER_GROUP = 4
NEG_BIG = -1e30
ZERO_ROWS = 64

VMEM_LIMIT = 56 * 1024 * 1024


def _cparams(sem):
    return pltpu.CompilerParams(dimension_semantics=sem, vmem_limit_bytes=VMEM_LIMIT)


def _dot(a, b):
    return jnp.dot(a, b, preferred_element_type=F32)


def _dot_nt(a, b):
    return lax.dot_general(a, b, (((1,), (1,)), ((), ())), preferred_element_type=F32)


def _silu(x):
    return x * jax.nn.sigmoid(x)


def _rms(x):
    return x * lax.rsqrt(jnp.mean(x * x, axis=-1, keepdims=True) + EPS)


def _ada_body(c_ref, w_ref, b_ref, o_ref):
    a = _silu(c_ref[...]).astype(BF16)
    o_ref[...] = _dot(a, w_ref[...].astype(BF16)) + b_ref[...]


def _ada(c, w, b):
    bsz, d = c.shape
    n = w.shape[1]
    rows = 16
    c16 = jnp.concatenate([c, jnp.zeros((rows - bsz, d), c.dtype)], axis=0)
    tn = 1024
    out = pl.pallas_call(
        _ada_body,
        grid=(n // tn,),
        in_specs=[pl.BlockSpec((rows, d), lambda j: (0, 0)),
                  pl.BlockSpec((d, tn), lambda j: (0, j)),
                  pl.BlockSpec((1, tn), lambda j: (0, j))],
        out_specs=pl.BlockSpec((rows, tn), lambda j: (0, j)),
        out_shape=jax.ShapeDtypeStruct((rows, n), F32),
        compiler_params=_cparams(("arbitrary",)),
        name="ada_mod",
    )(c16, w, b.reshape(1, n))
    return out[:bsz].reshape(bsz, 6, d)


def _norm_mod(x, g, sh, sc):
    return (_rms(x) * g) * (1.0 + sc) + sh


def _norm_body(x_ref, g_ref, mod_ref, o_ref, *, sh_idx, sc_idx):
    h = _norm_mod(x_ref[...], g_ref[...], mod_ref[0, sh_idx:sh_idx + 1, :], mod_ref[0, sc_idx:sc_idx + 1, :])
    o_ref[...] = h.astype(o_ref.dtype)


def _norm(x, g, mod, sh_idx, sc_idx, seq, tm=512):
    n, d = x.shape
    tpb = seq // tm
    return pl.pallas_call(
        functools.partial(_norm_body, sh_idx=sh_idx, sc_idx=sc_idx),
        grid=(n // tm,),
        in_specs=[pl.BlockSpec((tm, d), lambda i: (i, 0)),
                  pl.BlockSpec((1, d), lambda i: (0, 0)),
                  pl.BlockSpec((1, 6, d), lambda i: (i // tpb, 0, 0))],
        out_specs=pl.BlockSpec((tm, d), lambda i: (i, 0)),
        out_shape=jax.ShapeDtypeStruct((n, d), BF16),
        compiler_params=_cparams(("arbitrary",)),
        name="norm_mod",
    )(x, g.reshape(1, d), mod)


def _mm_body(a_ref, w_ref, o_ref):
    o_ref[...] = _dot(a_ref[...], w_ref[...]).astype(o_ref.dtype)


def _mm(a, w, tm=1024, tn=1024):
    n, k = a.shape
    nout = w.shape[1]
    tm = min(tm, n)
    return pl.pallas_call(
        _mm_body,
        grid=(n // tm, nout // tn),
        in_specs=[pl.BlockSpec((tm, k), lambda i, j: (i, 0)),
                  pl.BlockSpec((k, tn), lambda i, j: (0, j))],
        out_specs=pl.BlockSpec((tm, tn), lambda i, j: (i, j)),
        out_shape=jax.ShapeDtypeStruct((n, nout), BF16),
        compiler_params=_cparams(("arbitrary", "arbitrary")),
        name="proj_in",
    )(a, w)


def _mm_rope_body(a_ref, w_ref, cos_ref, sa_ref, sb_ref, o_ref, *, n_rope_tiles, tn):
    j = pl.program_id(1)

    @pl.when(j < n_rope_tiles)
    def _():
        cos, sa, sb = cos_ref[...], sa_ref[...], sb_ref[...]
        half = ROPE_DIM // 2
        a = a_ref[...]
        for c0 in range(0, tn, MXU_N):
            acc = _dot(a, w_ref[:, c0:c0 + MXU_N])
            for c in range(0, MXU_N, LANES):
                seg = acc[:, c:c + LANES]
                r = seg * cos + pltpu.roll(seg, LANES - half, 1) * sa + pltpu.roll(seg, half, 1) * sb
                o_ref[:, c0 + c:c0 + c + LANES] = r.astype(o_ref.dtype)

    @pl.when(j >= n_rope_tiles)
    def _():
        o_ref[...] = _dot(a_ref[...], w_ref[...]).astype(o_ref.dtype)


def _rope_tables(seq):
    half = ROPE_DIM // 2
    inv = ROPE_THETA ** (-jnp.arange(half, dtype=F32) / half)
    ang = jnp.arange(seq, dtype=F32)[:, None] * inv[None, :]
    cos, sin = jnp.cos(ang), jnp.sin(ang)
    pad = jnp.zeros((seq, LANES - ROPE_DIM), F32)
    z = jnp.zeros((seq, half), F32)
    cos_t = jnp.concatenate([cos, cos, pad + 1.0], axis=1)
    sa_t = jnp.concatenate([-sin, z, pad], axis=1)
    sb_t = jnp.concatenate([z, sin, pad], axis=1)
    return cos_t, sa_t, sb_t


def _mm_rope(a, w, seq, rope_cols, tm=1024, tn=1024):
    n, k = a.shape
    nout = w.shape[1]
    tm = min(tm, seq)
    tpb = seq // tm
    cos_t, sa_t, sb_t = _rope_tables(seq)
    tab_spec = pl.BlockSpec((tm, LANES), lambda i, j: (i % tpb, 0))
    return pl.pallas_call(
        functools.partial(_mm_rope_body, n_rope_tiles=rope_cols // tn, tn=tn),
        grid=(n // tm, nout // tn),
        in_specs=[pl.BlockSpec((tm, k), lambda i, j: (i, 0)),
                  pl.BlockSpec((k, tn), lambda i, j: (0, j)),
                  tab_spec, tab_spec, tab_spec],
        out_specs=pl.BlockSpec((tm, tn), lambda i, j: (i, j)),
        out_shape=jax.ShapeDtypeStruct((n, nout), BF16),
        compiler_params=_cparams(("arbitrary", "arbitrary")),
        name="proj_in_rope",
    )(a, w, cos_t, sa_t, sb_t)


def _mm_res_body(*refs, n_a, gt_idx):
    a_refs = refs[:n_a]
    w_ref, x_ref, mod_ref, o_ref = refs[n_a:]
    acc = None
    k0 = 0
    for a_ref in a_refs:
        kk = a_ref.shape[1]
        part = _dot(a_ref[...], w_ref[k0:k0 + kk, :])
        acc = part if acc is None else acc + part
        k0 += kk
    o_ref[...] = x_ref[...] + mod_ref[0, gt_idx:gt_idx + 1, :] * acc


def _mm_res(a_list, w, x, mod, gt_idx, seq, tm=512, tn=2048):
    n, d = x.shape
    tpb = seq // tm
    in_specs = [pl.BlockSpec((tm, a.shape[1]), lambda i, j: (i, 0)) for a in a_list]
    in_specs += [pl.BlockSpec((w.shape[0], tn), lambda i, j: (0, j)),
                 pl.BlockSpec((tm, tn), lambda i, j: (i, j)),
                 pl.BlockSpec((1, 6, tn), lambda i, j: (i // tpb, 0, j))]
    return pl.pallas_call(
        functools.partial(_mm_res_body, n_a=len(a_list), gt_idx=gt_idx),
        grid=(n // tm, d // tn),
        in_specs=in_specs,
        out_specs=pl.BlockSpec((tm, tn), lambda i, j: (i, j)),
        out_shape=jax.ShapeDtypeStruct((n, d), F32),
        compiler_params=_cparams(("arbitrary", "arbitrary")),
        name="proj_out_res",
    )(*a_list, w, x, mod)


def _chunk_prefix(g):
    r = g.shape[0]
    sub_rows = 8
    per_chunk = HG_CHUNK // sub_rows
    x = g.reshape(r // sub_rows, sub_rows, HG_D)
    sub = lax.broadcasted_iota(I32, x.shape, 1)
    for s in (1, 2, 4):
        x = x + jnp.where(sub >= s, pltpu.roll(x, s, 1), 0.0)
    tot = x[:, sub_rows - 1:sub_rows, :].reshape(r // HG_CHUNK, per_chunk, 1, HG_D)
    c1 = tot[:, 0:1]
    c2 = c1 + tot[:, 1:2]
    c3 = c2 + tot[:, 2:3]
    carry = jnp.concatenate([jnp.zeros_like(c1), c1, c2, c3], axis=1)
    return x.reshape(r // HG_CHUNK, per_chunk, sub_rows, HG_D) + carry, c3 + tot[:, 3:4]


def _hgrn_body(q_ref, i_ref, ff_ref, fb_ref, gg_ref, lbl_ref, ng_ref, o_ref,
               qg_s, kd_s, dec_s, vt_s, oacc_s, *, seq):
    blk = HG_BLK
    sblk = HG_SBLK
    cpb = blk // HG_CHUNK
    cps = sblk // HG_CHUNK
    lg = lbl_ref[...]
    ex = jnp.exp(lg - jnp.max(lg, axis=0, keepdims=True))
    lb = ex[0:1, :] / jnp.sum(ex, axis=0, keepdims=True)

    row = lax.broadcasted_iota(I32, (blk, blk), 0)
    col = lax.broadcasted_iota(I32, (blk, blk), 1)
    same_chunk = (row // HG_CHUNK) == (col // HG_CHUNK)
    masks = (same_chunk & (row >= col), same_chunk & (row <= col))
    srow = lax.broadcasted_iota(I32, (sblk, HG_D), 0) // HG_CHUNK
    f_refs = (ff_ref, fb_ref)
    shape4 = (cpb, HG_CHUNK // 8, 8, HG_D)
    lanes = [slice(hd * HG_D, (hd + 1) * HG_D) for hd in range(HG_HEADS_PER_STEP)]

    def stage1(b, carry):
        sl = pl.ds(pl.multiple_of(b * blk, blk), blk)
        for hd in range(HG_HEADS_PER_STEP):
            ln = lanes[hd]
            qh = _silu(q_ref[0, sl, ln].astype(F32)).reshape(shape4)
            v = i_ref[0, sl, ln]
            vf = v.astype(F32)
            for half in range(blk // sblk):
                vt_s[hd, b * (blk // sblk) + half] = vf[half * sblk:(half + 1) * sblk, :].T.astype(BF16)
            o_blk = None
            for d in range(2):
                f = lb[:, ln] + (1.0 - lb[:, ln]) * jax.nn.sigmoid(f_refs[d][0, sl, ln].astype(F32))
                g = jnp.log(f)
                k = (1.0 - f).reshape(shape4)
                pref, total = _chunk_prefix(g)
                gc = pref if d == 0 else total - pref + g.reshape(shape4)
                eg = jnp.exp(gc)
                dec = jnp.exp(total)
                qg = (qh * eg).reshape(blk, HG_D).astype(BF16)
                kgf = k * (1.0 / eg)
                kg = kgf.reshape(blk, HG_D).astype(BF16)
                qg_s[hd, d, sl, :] = qg
                kd_s[hd, d, sl, :] = (kgf * dec).reshape(blk, HG_D).astype(BF16)
                dec_s[hd, d, pl.ds(pl.multiple_of(b * cpb, cpb), cpb)] = dec.reshape(cpb, 1, HG_D)
                a = jnp.where(masks[d], _dot_nt(qg, kg), 0.0).astype(BF16)
                o = _dot(a, v)
                o_blk = o if o_blk is None else o_blk + o
            oacc_s[hd, sl, :] = o_blk
        return carry

    lax.fori_loop(0, seq // blk, stage1, 0)

    n_sblk = seq // sblk
    chains = [(hd, d) for hd in range(HG_HEADS_PER_STEP) for d in range(2)]

    def stage2(j, carry):
        states = list(carry)
        sbs = (j, n_sblk - 1 - j)
        uts = []
        for hd, d in chains:
            vt_b = vt_s[hd, sbs[d]]
            kd_b = kd_s[hd, d, pl.ds(pl.multiple_of(sbs[d] * sblk, sblk), sblk), :]
            uts.append([_dot(vt_b, jnp.where(srow == jj, kd_b, jnp.zeros_like(kd_b))) for jj in range(cps)])
        inter = [[None] * cps for _ in chains]
        for step in range(cps):
            for c, (hd, d) in enumerate(chains):
                jj = step if d == 0 else cps - 1 - step
                rows = pl.ds(pl.multiple_of(sbs[d] * sblk + jj * HG_CHUNK, HG_CHUNK), HG_CHUNK)
                inter[c][jj] = _dot_nt(qg_s[hd, d, rows, :], states[c].astype(BF16))
                states[c] = states[c] * dec_s[hd, d, sbs[d] * cps + jj] + uts[c][jj]
        for c, (hd, d) in enumerate(chains):
            rows = pl.ds(pl.multiple_of(sbs[d] * sblk, sblk), sblk)
            oacc_s[hd, rows, :] += jnp.concatenate(inter[c], axis=0)
        return tuple(states)

    zero = jnp.zeros((HG_D, HG_D), F32)
    lax.fori_loop(0, n_sblk, stage2, (zero,) * len(chains))

    for hd in range(HG_HEADS_PER_STEP):
        y = _rms(oacc_s[hd]) * ng_ref[hd]
        o_ref[0, :, lanes[hd]] = (y * _silu(gg_ref[0, :, lanes[hd]].astype(F32))).astype(o_ref.dtype)


def _hgrn(proj, lb_logits, norm_g):
    bsz, seq, _ = proj.shape
    n_chunks = seq // HG_CHUNK
    n_layers = lb_logits.shape[0]

    hps = HG_HEADS_PER_STEP
    steps = HG_HEADS // hps
    width = hps * HG_D

    def seg_spec(seg):
        return pl.BlockSpec((1, seq, width), lambda b, h: (b, 0, seg * steps + h))

    return pl.pallas_call(
        functools.partial(_hgrn_body, seq=seq),
        grid=(bsz, steps),
        in_specs=[seg_spec(0), seg_spec(1), seg_spec(2), seg_spec(3), seg_spec(4),
                  pl.BlockSpec((n_layers, width), lambda b, h: (0, h)),
                  pl.BlockSpec((hps, 1, HG_D), lambda b, h: (h, 0, 0))],
        out_specs=pl.BlockSpec((1, seq, width), lambda b, h: (b, 0, h)),
        out_shape=jax.ShapeDtypeStruct((bsz, seq, HG_HEADS * HG_D), BF16),
        scratch_shapes=[pltpu.VMEM((hps, 2, seq, HG_D), BF16), pltpu.VMEM((hps, 2, seq, HG_D), BF16),
                        pltpu.VMEM((hps, 2, n_chunks, 1, HG_D), F32),
                        pltpu.VMEM((hps, seq // HG_SBLK, HG_D, HG_SBLK), BF16),
                        pltpu.VMEM((hps, seq, HG_D), F32)],
        compiler_params=_cparams(("arbitrary", "arbitrary")),
        name="hgrn2",
    )(proj, proj, proj, proj, proj, lb_logits, norm_g.reshape(HG_HEADS, 1, HG_D))


def _na_bias_table(rpb):
    col = jnp.arange(GRID_W)
    dc = jnp.clip(col[None, :] - col[:, None] + (NA_KW - 1), 0, 2 * NA_KW - 2)
    sel = (dc[None] == jnp.arange(2 * NA_KW - 1)[:, None, None]).astype(F32)
    toe = jnp.einsum("hrd,dqk->hrqk", rpb.astype(F32), sel, precision=lax.Precision.HIGHEST)
    pats = []
    for p in range(NA_KH):
        w = toe[:, NA_KH - 1 - p:2 * NA_KH - 1 - p]
        pats.append(w.transpose(0, 2, 1, 3).reshape(rpb.shape[0], GRID_W, NA_KH * GRID_W))
    return jnp.stack(pats, axis=1)


def _na_body(q_ref, k_ref, v_ref, bt_ref, o_ref, *, seq):
    rows = seq // GRID_W
    win = NA_KH * GRID_W
    cq = lax.broadcasted_iota(I32, (GRID_W, win), 0)
    ck = lax.broadcasted_iota(I32, (GRID_W, win), 1) % GRID_W
    cs = jnp.clip(cq - NA_KW // 2, 0, GRID_W - NA_KW)
    in_win = (ck >= cs) & (ck < cs + NA_KW)
    scale = NA_DH ** -0.5
    heads_here = LANES // NA_DH
    lane_head = lax.broadcasted_iota(I32, (GRID_W, LANES), 1) // NA_DH
    out_head = lane_head

    def row_group(rg, carry):
        items = []
        for u in range(NA_ROWS_PER_STEP):
            r = rg * NA_ROWS_PER_STEP + u
            start = jnp.clip(r - NA_KH // 2, 0, rows - NA_KH)
            qs = pl.ds(pl.multiple_of(r * GRID_W, GRID_W), GRID_W)
            ks = pl.ds(pl.multiple_of(start * GRID_W, GRID_W), win)
            q = q_ref[0, qs, :]
            k = k_ref[0, ks, :]
            for hh in range(heads_here):
                qh = jnp.where(lane_head == hh, q, jnp.zeros_like(q))
                items.append((_dot_nt(qh, k), hh, r - start, qs, ks))
        probs = []
        for s, hh, pat, qs, ks in items:
            s = jnp.where(in_win, s * scale + bt_ref[hh, pat], NEG_BIG)
            e = jnp.exp(s - jnp.max(s, axis=-1, keepdims=True))
            probs.append((e.astype(BF16), 1.0 / jnp.sum(e, axis=-1, keepdims=True), qs, ks))
        for i in range(0, len(probs), heads_here):
            qs, ks = probs[i][2], probs[i][3]
            v = v_ref[0, ks, :]
            o = None
            for hh in range(heads_here):
                oh = _dot(probs[i + hh][0], v) * probs[i + hh][1]
                o = oh if o is None else jnp.where(out_head == hh, oh, o)
            o_ref[0, qs, :] = o.astype(o_ref.dtype)
        return carry

    lax.fori_loop(0, rows // NA_ROWS_PER_STEP, row_group, 0)


def _na(proj, rpb, col0):
    bsz, seq, _ = proj.shape
    hpb = LANES // NA_DH
    nblk = NA_HEADS // hpb
    b0 = col0 // LANES
    bt = _na_bias_table(rpb)

    def seg_spec(seg):
        return pl.BlockSpec((1, seq, LANES), lambda b, h: (b, 0, b0 + seg * nblk + h))

    return pl.pallas_call(
        functools.partial(_na_body, seq=seq),
        grid=(bsz, nblk),
        in_specs=[seg_spec(0), seg_spec(1), seg_spec(2),
                  pl.BlockSpec((hpb, NA_KH, GRID_W, NA_KH * GRID_W), lambda b, h: (h, 0, 0, 0))],
        out_specs=pl.BlockSpec((1, seq, LANES), lambda b, h: (b, 0, h)),
        out_shape=jax.ShapeDtypeStruct((bsz, seq, NA_HEADS * NA_DH), BF16),
        compiler_params=_cparams(("arbitrary", "arbitrary")),
        name="natten",
    )(proj, proj, proj, bt)


def _da_body(q_ref, k_ref, v_ref, lam_ref, g_ref, o_ref, *, lam_init):
    lp = lam_ref[...]
    lam = (jnp.exp(jnp.sum(lp[0:1] * lp[1:2], axis=-1, keepdims=True))
           - jnp.exp(jnp.sum(lp[2:3] * lp[3:4], axis=-1, keepdims=True)) + lam_init)
    scale2 = (DA_DH ** -0.5) * math.log2(math.e)
    v = v_ref[0]

    def unnormalised(s):
        e = jnp.exp2(s * scale2 - jnp.max(s, axis=-1, keepdims=True) * scale2)
        return e, 1.0 / jnp.sum(e, axis=-1, keepdims=True)

    rows_per = q_ref.shape[1] // DA_ROW_PIECES
    pieces = [slice(p * rows_per, (p + 1) * rows_per) for p in range(DA_ROW_PIECES)]
    ss = [[_dot_nt(q_ref[0, rs, lo:lo + DA_DH], k_ref[0, :, lo:lo + DA_DH]) for lo in (0, DA_DH)]
          for rs in pieces]
    for rs, (s1, s2) in zip(pieces, ss):
        e1, r1 = unnormalised(s1)
        e2, r2 = unnormalised(s2)
        o = _dot((e1 * r1 - e2 * (lam * r2)).astype(BF16), v)
        o_ref[0, rs, :] = (_rms(o) * g_ref[...] * (1.0 - lam_init)).astype(o_ref.dtype)


def _da(proj, diff_lambda, diff_norm, layer_idx, tq=512):
    bsz, seq, _ = proj.shape
    tq = min(tq, seq)
    dv = 2 * DA_DH
    lam_init = 0.8 - 0.6 * math.exp(-0.3 * layer_idx)
    return pl.pallas_call(
        functools.partial(_da_body, lam_init=lam_init),
        grid=(bsz, DA_HEADS, seq // tq),
        in_specs=[pl.BlockSpec((1, tq, dv), lambda b, h, i: (b, i, h)),
                  pl.BlockSpec((1, seq, dv), lambda b, h, i: (b, 0, DA_HEADS + h)),
                  pl.BlockSpec((1, seq, dv), lambda b, h, i: (b, 0, 2 * DA_HEADS + h)),
                  pl.BlockSpec((4, DA_DH), lambda b, h, i: (0, 0)),
                  pl.BlockSpec((1, dv), lambda b, h, i: (0, 0))],
        out_specs=pl.BlockSpec((1, tq, dv), lambda b, h, i: (b, i, h)),
        out_shape=jax.ShapeDtypeStruct((bsz, seq, DA_HEADS * dv), BF16),
        compiler_params=_cparams(("arbitrary", "arbitrary", "arbitrary")),
        name="diff_attn",
    )(proj, proj, proj, diff_lambda, diff_norm.reshape(1, dv))


def _route_body(x_ref, g_ref, mod_ref, rwc_ref, rwh_ref, rb_ref,
                h_ref, e_ref, gate_ref, rank_ref, cnt_ref, run_s, *, tm):
    i = pl.program_id(0)

    @pl.when(i == 0)
    def _():
        run_s[...] = jnp.zeros_like(run_s)

    h = _norm_mod(x_ref[...], g_ref[...], mod_ref[0, 3:4, :], mod_ref[0, 4:5, :])
    h_ref[...] = h
    h_hi = h.astype(BF16)
    h_lo = (h - h_hi.astype(F32)).astype(BF16)
    both = _dot_nt(rwc_ref[...], h_hi)
    logits = both[0:N_EXPERTS] + both[N_EXPERTS:2 * N_EXPERTS] + _dot_nt(rwh_ref[...], h_lo)
    scores = jax.nn.sigmoid(logits)
    sel = scores + rb_ref[...]

    sc = [scores[e:e + 1, :] for e in range(N_EXPERTS)]
    sl = [sel[e:e + 1, :] for e in range(N_EXPERTS)]

    def top2_sum(a, b, c, d):
        hi1, lo1 = jnp.maximum(a, b), jnp.minimum(a, b)
        hi2, lo2 = jnp.maximum(c, d), jnp.minimum(c, d)
        return jnp.maximum(hi1, hi2) + jnp.maximum(jnp.minimum(hi1, hi2), jnp.maximum(lo1, lo2))

    gs = [top2_sum(*sl[4 * g:4 * g + 4]) for g in range(N_GROUPS)]
    best = jnp.zeros_like(gs[0], dtype=I32)
    best_v = gs[0]
    for g in range(1, N_GROUPS):
        better = gs[g] > best_v
        best = jnp.where(better, g, best)
        best_v = jnp.where(better, gs[g], best_v)

    def pick(vals, j):
        out = vals[j]
        for g in range(1, N_GROUPS):
            out = jnp.where(best == g, vals[4 * g + j], out)
        return out

    v = [pick(sl, j) for j in range(EXPERTS_PER_GROUP)]
    w = [pick(sc, j) for j in range(EXPERTS_PER_GROUP)]

    def argmax4(vals, excluded):
        idx = jnp.full_like(best, -1)
        cur = jnp.full_like(vals[0], -jnp.inf)
        for j in range(EXPERTS_PER_GROUP):
            ok = (vals[j] > cur) if excluded is None else ((vals[j] > cur) & (excluded != j))
            idx = jnp.where(ok, j, idx)
            cur = jnp.where(ok, vals[j], cur)
        return idx

    i1 = argmax4(v, None)
    i2 = argmax4(v, i1)

    def take(vals, idx):
        out = vals[0]
        for j in range(1, EXPERTS_PER_GROUP):
            out = jnp.where(idx == j, vals[j], out)
        return out

    w1, w2 = take(w, i1), take(w, i2)
    den = w1 + w2
    e1 = best * EXPERTS_PER_GROUP + i1
    e2 = best * EXPERTS_PER_GROUP + i2
    e_ref[0:1, :] = e1
    e_ref[1:2, :] = e2
    gate_ref[0:1, :] = w1 / den
    gate_ref[1:2, :] = w2 / den

    eid = lax.broadcasted_iota(I32, (N_EXPERTS, tm), 0)
    oh1 = eid == e1
    oh2 = eid == e2
    onehot = (oh1 | oh2).astype(BF16)
    srow = lax.broadcasted_iota(I32, (tm, tm), 0)
    tcol = lax.broadcasted_iota(I32, (tm, tm), 1)
    before = (srow < tcol).astype(BF16)
    pos = _dot(onehot, before) + run_s[...]
    rank_ref[0:1, :] = jnp.sum(jnp.where(oh1, pos, 0.0), axis=0, keepdims=True).astype(I32)
    rank_ref[1:2, :] = jnp.sum(jnp.where(oh2, pos, 0.0), axis=0, keepdims=True).astype(I32)
    run_s[...] = run_s[...] + jnp.sum(onehot.astype(F32), axis=1, keepdims=True)
    cnt_ref[...] = jnp.broadcast_to(run_s[...], cnt_ref.shape).astype(I32)


def _route(x, g, mod, router_w, router_b, seq, tm=512):
    n, d = x.shape
    tpb = seq // tm
    rwt = router_w.T
    rw_hi = rwt.astype(BF16)
    rw_lo = (rwt - rw_hi.astype(F32)).astype(BF16)
    rwc = jnp.concatenate([rw_hi, rw_lo], axis=0)
    pair_spec = pl.BlockSpec((2, tm), lambda i: (0, i))
    return pl.pallas_call(
        functools.partial(_route_body, tm=tm),
        grid=(n // tm,),
        in_specs=[pl.BlockSpec((tm, d), lambda i: (i, 0)),
                  pl.BlockSpec((1, d), lambda i: (0, 0)),
                  pl.BlockSpec((1, 6, d), lambda i: (i // tpb, 0, 0)),
                  pl.BlockSpec((2 * N_EXPERTS, d), lambda i: (0, 0)),
                  pl.BlockSpec((N_EXPERTS, d), lambda i: (0, 0)),
                  pl.BlockSpec((N_EXPERTS, 1), lambda i: (0, 0))],
        out_specs=[pl.BlockSpec((tm, d), lambda i: (i, 0)), pair_spec, pair_spec, pair_spec,
                   pl.BlockSpec((N_EXPERTS, LANES), lambda i: (0, 0))],
        out_shape=[jax.ShapeDtypeStruct((n, d), F32),
                   jax.ShapeDtypeStruct((2, n), I32),
                   jax.ShapeDtypeStruct((2, n), F32),
                   jax.ShapeDtypeStruct((2, n), I32),
                   jax.ShapeDtypeStruct((N_EXPERTS, LANES), I32)],
        scratch_shapes=[pltpu.VMEM((N_EXPERTS, 1), F32)],
        compiler_params=_cparams(("arbitrary",)),
        name="moe_route",
    )(x, g.reshape(1, d), mod, rwc, rw_hi, router_b.reshape(N_EXPERTS, 1).astype(F32))


def _row_copy(src_ref, s, dst_ref, d, sem):
    return pltpu.make_async_copy(src_ref.at[pl.ds(s, 1), :], dst_ref.at[pl.ds(d, 1), :], sem)


def _dispatch_body(meta_ref, dest_ref, h_ref, xs_ref, zero_s, sem, *, tmd, tm, max_tiles):
    i = pl.program_id(0)
    zr = zero_s.shape[0]

    def zero_tile(start):
        copies = [pltpu.make_async_copy(zero_s, xs_ref.at[pl.ds(pl.multiple_of(start + z * zr, zr), zr), :], sem)
                  for z in range(tm // zr)]
        for cp in copies:
            cp.start()
        for cp in copies:
            cp.wait()

    @pl.when(i == 0)
    def _():
        zero_s[...] = jnp.zeros_like(zero_s)
        for e in range(N_EXPERTS):
            @pl.when(meta_ref[N_EXPERTS + e] > 0)
            def _():
                zero_tile(pl.multiple_of(meta_ref[e] - tm, tm))

        def fill_unused(t, carry):
            zero_tile(pl.multiple_of(t * tm, tm))
            return carry

        lax.fori_loop(meta_ref[2 * N_EXPERTS], max_tiles, fill_unused, 0)

    def issue(t, carry):
        _row_copy(h_ref, t, xs_ref, dest_ref[0, 0, t], sem).start(priority=0)
        _row_copy(h_ref, t, xs_ref, dest_ref[0, 0, tmd + t], sem).start(priority=1)
        return carry

    lax.fori_loop(0, tmd, issue, 0, unroll=4)

    def drain(t, carry):
        _row_copy(h_ref, 0, xs_ref, 0, sem).wait()
        _row_copy(h_ref, 0, xs_ref, 0, sem).wait()
        return carry

    lax.fori_loop(0, tmd, drain, 0)


def _dispatch(h, dest, meta, n_slots, tm, tmd=256):
    n, d = h.shape
    dest_blk = jnp.concatenate([dest[0].reshape(n // tmd, 1, tmd), dest[1].reshape(n // tmd, 1, tmd)], axis=2)
    grid_spec = pltpu.PrefetchScalarGridSpec(
        num_scalar_prefetch=1,
        grid=(n // tmd,),
        in_specs=[pl.BlockSpec((1, 1, 2 * tmd), lambda i, m: (i, 0, 0), memory_space=pltpu.SMEM),
                  pl.BlockSpec((tmd, d), lambda i, m: (i, 0))],
        out_specs=pl.BlockSpec(memory_space=pl.ANY),
        scratch_shapes=[pltpu.VMEM((ZERO_ROWS, d), F32), pltpu.SemaphoreType.DMA(())],
    )
    return pl.pallas_call(
        functools.partial(_dispatch_body, tmd=tmd, tm=tm, max_tiles=n_slots // tm),
        grid_spec=grid_spec,
        out_shape=jax.ShapeDtypeStruct((n_slots, d), F32),
        compiler_params=_cparams(("arbitrary",)),
        name="moe_dispatch",
    )(meta, dest_blk, h)


def _moe_up_body(te_ref, nv_ref, xs_ref, wg_ref, wu_ref, hid_ref, wgb_s, wub_s):
    i = pl.program_id(1)
    valid = i < nv_ref[0]
    changed = (i == 0) | (te_ref[i] != te_ref[jnp.maximum(i - 1, 0)])

    @pl.when(valid & changed)
    def _():
        wgb_s[...] = wg_ref[0].astype(BF16)
        wub_s[...] = wu_ref[0].astype(BF16)

    @pl.when(valid)
    def _():
        x = xs_ref[...].astype(BF16)
        a = _dot(x, wgb_s[...])
        u = _dot(x, wub_s[...])
        hid_ref[...] = (_silu(a) * u).astype(hid_ref.dtype)

    @pl.when(jnp.logical_not(valid))
    def _():
        hid_ref[...] = jnp.zeros_like(hid_ref)


def _moe_up(xs, wg, wu, te, nv, tm, fc=512):
    n_slots, d = xs.shape
    ff = wg.shape[2]
    max_tiles = n_slots // tm

    def row_map(j, i, te_r, nv_r):
        return (jnp.minimum(i, nv_r[0] - 1), 0)

    def w_map(j, i, te_r, nv_r):
        return (te_r[i], 0, j)

    grid_spec = pltpu.PrefetchScalarGridSpec(
        num_scalar_prefetch=2,
        grid=(ff // fc, max_tiles),
        in_specs=[pl.BlockSpec((tm, d), row_map),
                  pl.BlockSpec((1, d, fc), w_map),
                  pl.BlockSpec((1, d, fc), w_map)],
        out_specs=pl.BlockSpec((tm, fc), lambda j, i, te_r, nv_r: (i, j)),
        scratch_shapes=[pltpu.VMEM((d, fc), BF16), pltpu.VMEM((d, fc), BF16)],
    )
    return pl.pallas_call(
        _moe_up_body,
        grid_spec=grid_spec,
        out_shape=jax.ShapeDtypeStruct((n_slots, ff), BF16),
        compiler_params=_cparams(("arbitrary", "arbitrary")),
        name="moe_up",
    )(te, nv, xs, wg, wu)


def _moe_down_body(te_ref, nv_ref, hid_ref, wd_ref, o_ref, wdb_s):
    i = pl.program_id(1)
    valid = i < nv_ref[0]
    changed = (i == 0) | (te_ref[i] != te_ref[jnp.maximum(i - 1, 0)])

    @pl.when(valid & changed)
    def _():
        wdb_s[...] = wd_ref[0].astype(BF16)

    @pl.when(valid)
    def _():
        o_ref[...] = _dot(hid_ref[...], wdb_s[...])

    @pl.when(jnp.logical_not(valid))
    def _():
        o_ref[...] = jnp.zeros_like(o_ref)


def _moe_down(hid, wd, te, nv, tm, nc=2048):
    n_slots, ff = hid.shape
    d = wd.shape[2]
    max_tiles = n_slots // tm
    grid_spec = pltpu.PrefetchScalarGridSpec(
        num_scalar_prefetch=2,
        grid=(d // nc, max_tiles),
        in_specs=[pl.BlockSpec((tm, ff), lambda j, i, te_r, nv_r: (jnp.minimum(i, nv_r[0] - 1), 0)),
                  pl.BlockSpec((1, ff, nc), lambda j, i, te_r, nv_r: (te_r[i], 0, j))],
        out_specs=pl.BlockSpec((tm, nc), lambda j, i, te_r, nv_r: (i, j)),
        scratch_shapes=[pltpu.VMEM((ff, nc), BF16)],
    )
    return pl.pallas_call(
        _moe_down_body,
        grid_spec=grid_spec,
        out_shape=jax.ShapeDtypeStruct((n_slots, d), F32),
        compiler_params=_cparams(("arbitrary", "arbitrary")),
        name="moe_down",
    )(te, nv, hid, wd)


def _combine_body(dest_ref, x_ref, gate_ref, mod_ref, ys_ref, *rest, tc, final, with_next):
    hn_ref = None
    if final:
        lnf_ref, o_ref, buf_s, sem = rest
    elif with_next:
        lng_ref, modn_ref, o_ref, hn_ref, buf_s, sem = rest
    else:
        o_ref, buf_s, sem = rest

    i = pl.program_id(0)
    n_tiles = pl.num_programs(0) - 1
    slot = i % 2

    def row_in(s, k, t, d):
        return pltpu.make_async_copy(ys_ref.at[pl.ds(d, 1), :], buf_s.at[s, k, pl.ds(t, 1), :], sem.at[s])

    @pl.when(i < n_tiles)
    def _():
        def issue(t, carry):
            row_in(slot, 0, t, dest_ref[0, 0, t]).start(priority=0)
            row_in(slot, 1, t, dest_ref[0, 0, tc + t]).start(priority=1)
            return carry

        lax.fori_loop(0, tc, issue, 0, unroll=4)

    @pl.when(i > 0)
    def _():
        prev = 1 - slot

        def drain(t, carry):
            row_in(prev, 0, 0, 0).wait()
            row_in(prev, 1, 0, 0).wait()
            return carry

        lax.fori_loop(0, tc, drain, 0)

        gates = gate_ref[...]
        y = x_ref[...] + mod_ref[0, 5:6, :] * (gates[:, 0:1] * buf_s[prev, 0] + gates[:, 1:2] * buf_s[prev, 1])
        if final:
            y = _rms(y) * lnf_ref[...]
        o_ref[...] = y
        if hn_ref is not None:
            hn_ref[...] = _norm_mod(y, lng_ref[...], modn_ref[0, 0:1, :], modn_ref[0, 1:2, :]).astype(hn_ref.dtype)


def _combine(x, ys, dest, gates, mod, seq, ln_final=None, next_norm=None, tc=256):
    n, d = x.shape
    tpb = seq // tc
    final = ln_final is not None
    with_next = next_norm is not None
    n_tiles = n // tc
    dest_blk = jnp.concatenate([dest[0].reshape(n_tiles, 1, tc), dest[1].reshape(n_tiles, 1, tc)], axis=2)

    def done(i):
        return jnp.maximum(i - 1, 0)

    row_spec = pl.BlockSpec((tc, d), lambda i: (done(i), 0))
    mod_spec = pl.BlockSpec((1, 6, d), lambda i: (done(i) // tpb, 0, 0))
    vec_spec = pl.BlockSpec((1, d), lambda i: (0, 0))
    in_specs = [pl.BlockSpec((1, 1, 2 * tc), lambda i: (jnp.minimum(i, n_tiles - 1), 0, 0), memory_space=pltpu.SMEM),
                row_spec,
                pl.BlockSpec((tc, 2), lambda i: (done(i), 0)),
                mod_spec,
                pl.BlockSpec(memory_space=pl.ANY)]
    args = [dest_blk, x, gates.T, mod, ys]
    if final:
        in_specs.append(vec_spec)
        args.append(ln_final.reshape(1, d))
    out_specs, out_shape = row_spec, jax.ShapeDtypeStruct((n, d), F32)
    if with_next:
        in_specs += [vec_spec, mod_spec]
        args += [next_norm[0].reshape(1, d), next_norm[1]]
        out_specs, out_shape = [row_spec, row_spec], [out_shape, jax.ShapeDtypeStruct((n, d), BF16)]
    return pl.pallas_call(
        functools.partial(_combine_body, tc=tc, final=final, with_next=with_next),
        grid=(n_tiles + 1,),
        in_specs=in_specs,
        out_specs=out_specs,
        out_shape=out_shape,
        scratch_shapes=[pltpu.VMEM((2, 2, tc, d), F32), pltpu.SemaphoreType.DMA((2,))],
        compiler_params=_cparams(("arbitrary",)),
        name="moe_combine",
    )(*args)


def _moe(x, ln_g, mod, router_w, router_b, wg, wu, wd, seq, ln_final=None, next_norm=None, tm=512):
    n, d = x.shape
    h, eidx, gates, rank, cnt = _route(x, ln_g, mod, router_w, router_b, seq)
    counts = cnt[:, 0]
    ntile = (counts + tm - 1) // tm
    cum = jnp.cumsum(ntile)
    off = (cum - ntile) * tm
    max_tiles = (2 * n) // tm + N_EXPERTS
    nv = cum[-1:].astype(I32)
    tiles = jnp.minimum(jnp.arange(max_tiles, dtype=I32), nv[0] - 1)
    te = jnp.minimum(jnp.sum((tiles[:, None] >= cum[None, :]).astype(I32), axis=1), N_EXPERTS - 1)
    expert_ids = jnp.arange(N_EXPERTS, dtype=I32)[:, None, None]
    dest = rank + jnp.sum(jnp.where(eidx[None] == expert_ids, off[:, None, None], 0), axis=0)
    dest = dest.astype(I32)
    meta = jnp.concatenate([cum * tm, ntile, nv]).astype(I32)
    xs = _dispatch(h, dest, meta, max_tiles * tm, tm)
    hid = _moe_up(xs, wg, wu, te, nv, tm)
    ys = _moe_down(hid, wd, te, nv, tm)
    return _combine(x, ys, dest, gates, mod, seq, ln_final, next_norm)


def kernel(x, c, ln_attn_0, ada_w_0, ada_b_0, w_in_0, hgrn_norm_0, na_rpb_0, w_out_0, ln_ffn_0, moe_wg_0, moe_wu_0, moe_wd_0, ln_attn_1, ada_w_1, ada_b_1, w_in_1, diff_lambda_1, diff_norm_1, w_out_1, ln_ffn_1, moe_wg_1, moe_wu_1, moe_wd_1, hgrn_lb_logits, router_w, router_b, ln_final):
    bsz, seq, d = x.shape
    n = bsz * seq
    xf = x.reshape(n, d)
    mod0 = _ada(c, ada_w_0, ada_b_0)
    mod1 = _ada(c, ada_w_1, ada_b_1)

    h = _norm(xf, ln_attn_0, mod0, 0, 1, seq)
    proj = _mm(h, w_in_0.astype(BF16)).reshape(bsz, seq, -1)
    o_a = _hgrn(proj, hgrn_lb_logits, hgrn_norm_0)
    o_b = _na(proj, na_rpb_0, 5 * HG_HEADS * HG_D)
    xf = _mm_res([o_a.reshape(n, -1), o_b.reshape(n, -1)], w_out_0.astype(BF16), xf, mod0, 2, seq)
    xf, h = _moe(xf, ln_ffn_0, mod0, router_w, router_b, moe_wg_0, moe_wu_0, moe_wd_0, seq,
                 next_norm=(ln_attn_1, mod1))

    proj = _mm_rope(h, w_in_1.astype(BF16), seq, 4 * DA_HEADS * DA_DH).reshape(bsz, seq, -1)
    o_c = _da(proj, diff_lambda_1, diff_norm_1, 1)
    xf = _mm_res([o_c.reshape(n, -1)], w_out_1.astype(BF16), xf, mod1, 2, seq)
    xf = _moe(xf, ln_ffn_1, mod1, router_w, router_b, moe_wg_1, moe_wu_1, moe_wd_1, seq, ln_final=ln_final)
    return xf.reshape(bsz, seq, d)
```

```python
import functools
import math

import jax
import jax.numpy as jnp
from jax import lax
from jax.experimental import pallas as pl
from jax.experimental.pallas import tpu as pltpu

F32 = jnp.float32
BF16 = jnp.bfloat16
I32 = jnp.int32

EPS = 1e-6
LANES = 128
MXU_N = 256
GRID_W = 64
HG_HEADS = 8
HG_D = 128
HG_CHUNK = 32
HG_BLK = 256
HG_SBLK = 128
HG_HEADS_PER_STEP = 4
NA_HEADS = 16
NA_DH = 64
NA_KH = 8
NA_KW = 16
NA_ROWS_PER_STEP = 8
DA_HEADS = 8
DA_DH = 128
DA_ROW_PIECES = 4
ROPE_THETA = 500000.0
ROPE_DIM = DA_DH // 4
N_EXPERTS = 16
N_GROUPS = 4
EXPERTS_PER_GROUP = 4
NEG_BIG = -1e30
ZERO_ROWS = 64

VMEM_LIMIT = 56 * 1024 * 1024


def _cparams(sem):
    return pltpu.CompilerParams(dimension_semantics=sem, vmem_limit_bytes=VMEM_LIMIT)


def _dot(a, b):
    return jnp.dot(a, b, preferred_element_type=F32)


def _dot_nt(a, b):
    return lax.dot_general(a, b, (((1,), (1,)), ((), ())), preferred_element_type=F32)


def _silu(x):
    return x * jax.nn.sigmoid(x)


def _rms(x):
    return x * lax.rsqrt(jnp.mean(x * x, axis=-1, keepdims=True) + EPS)


def _ada_body(c_ref, w_ref, b_ref, o_ref):
    a = _silu(c_ref[...]).astype(BF16)
    o_ref[...] = _dot(a, w_ref[...].astype(BF16)) + b_ref[...]


def _ada(c, w, b):
    bsz, d = c.shape
    n = w.shape[1]
    rows = 16
    c16 = jnp.concatenate([c, jnp.zeros((rows - bsz, d), c.dtype)], axis=0)
    tn = 1024
    out = pl.pallas_call(
        _ada_body,
        grid=(n // tn,),
        in_specs=[pl.BlockSpec((rows, d), lambda j: (0, 0)),
                  pl.BlockSpec((d, tn), lambda j: (0, j)),
                  pl.BlockSpec((1, tn), lambda j: (0, j))],
        out_specs=pl.BlockSpec((rows, tn), lambda j: (0, j)),
        out_shape=jax.ShapeDtypeStruct((rows, n), F32),
        compiler_params=_cparams(("arbitrary",)),
        name="ada_mod",
    )(c16, w, b.reshape(1, n))
    return out[:bsz].reshape(bsz, 6, d)


def _norm_mod(x, g, sh, sc):
    return (_rms(x) * g) * (1.0 + sc) + sh


def _norm_body(x_ref, g_ref, mod_ref, o_ref, *, sh_idx, sc_idx):
    h = _norm_mod(x_ref[...], g_ref[...], mod_ref[0, sh_idx:sh_idx + 1, :], mod_ref[0, sc_idx:sc_idx + 1, :])
    o_ref[...] = h.astype(o_ref.dtype)


def _norm(x, g, mod, sh_idx, sc_idx, seq, tm=512):
    n, d = x.shape
    tpb = seq // tm
    return pl.pallas_call(
        functools.partial(_norm_body, sh_idx=sh_idx, sc_idx=sc_idx),
        grid=(n // tm,),
        in_specs=[pl.BlockSpec((tm, d), lambda i: (i, 0)),
                  pl.BlockSpec((1, d), lambda i: (0, 0)),
                  pl.BlockSpec((1, 6, d), lambda i: (i // tpb, 0, 0))],
        out_specs=pl.BlockSpec((tm, d), lambda i: (i, 0)),
        out_shape=jax.ShapeDtypeStruct((n, d), BF16),
        compiler_params=_cparams(("arbitrary",)),
        name="norm_mod",
    )(x, g.reshape(1, d), mod)


def _mm_body(a_ref, w_ref, o_ref):
    o_ref[...] = _dot(a_ref[...], w_ref[...]).astype(o_ref.dtype)


def _mm(a, w, tm=1024, tn=1024):
    n, k = a.shape
    nout = w.shape[1]
    tm = min(tm, n)
    return pl.pallas_call(
        _mm_body,
        grid=(n // tm, nout // tn),
        in_specs=[pl.BlockSpec((tm, k), lambda i, j: (i, 0)),
                  pl.BlockSpec((k, tn), lambda i, j: (0, j))],
        out_specs=pl.BlockSpec((tm, tn), lambda i, j: (i, j)),
        out_shape=jax.ShapeDtypeStruct((n, nout), BF16),
        compiler_params=_cparams(("arbitrary", "arbitrary")),
        name="proj_in",
    )(a, w)


def _mm_rope_body(a_ref, w_ref, cos_ref, sa_ref, sb_ref, o_ref, *, n_rope_tiles, tn):
    j = pl.program_id(1)

    @pl.when(j < n_rope_tiles)
    def _():
        cos, sa, sb = cos_ref[...], sa_ref[...], sb_ref[...]
        half = ROPE_DIM // 2
        a = a_ref[...]
        for c0 in range(0, tn, MXU_N):
            acc = _dot(a, w_ref[:, c0:c0 + MXU_N])
            for c in range(0, MXU_N, LANES):
                seg = acc[:, c:c + LANES]
                r = seg * cos + pltpu.roll(seg, LANES - half, 1) * sa + pltpu.roll(seg, half, 1) * sb
                o_ref[:, c0 + c:c0 + c + LANES] = r.astype(o_ref.dtype)

    @pl.when(j >= n_rope_tiles)
    def _():
        o_ref[...] = _dot(a_ref[...], w_ref[...]).astype(o_ref.dtype)


def _rope_tables(seq):
    half = ROPE_DIM // 2
    inv = ROPE_THETA ** (-jnp.arange(half, dtype=F32) / half)
    ang = jnp.arange(seq, dtype=F32)[:, None] * inv[None, :]
    cos, sin = jnp.cos(ang), jnp.sin(ang)
    pad = jnp.zeros((seq, LANES - ROPE_DIM), F32)
    z = jnp.zeros((seq, half), F32)
    cos_t = jnp.concatenate([cos, cos, pad + 1.0], axis=1)
    sa_t = jnp.concatenate([-sin, z, pad], axis=1)
    sb_t = jnp.concatenate([z, sin, pad], axis=1)
    return cos_t, sa_t, sb_t


def _mm_rope(a, w, seq, rope_cols, tm=1024, tn=1024):
    n, k = a.shape
    nout = w.shape[1]
    tm = min(tm, seq)
    tpb = seq // tm
    cos_t, sa_t, sb_t = _rope_tables(seq)
    tab_spec = pl.BlockSpec((tm, LANES), lambda i, j: (i % tpb, 0))
    return pl.pallas_call(
        functools.partial(_mm_rope_body, n_rope_tiles=rope_cols // tn, tn=tn),
        grid=(n // tm, nout // tn),
        in_specs=[pl.BlockSpec((tm, k), lambda i, j: (i, 0)),
                  pl.BlockSpec((k, tn), lambda i, j: (0, j)),
                  tab_spec, tab_spec, tab_spec],
        out_specs=pl.BlockSpec((tm, tn), lambda i, j: (i, j)),
        out_shape=jax.ShapeDtypeStruct((n, nout), BF16),
        compiler_params=_cparams(("arbitrary", "arbitrary")),
        name="proj_in_rope",
    )(a, w, cos_t, sa_t, sb_t)


def _mm_res_body(*refs, n_a, gt_idx):
    a_refs = refs[:n_a]
    w_ref, x_ref, mod_ref, o_ref = refs[n_a:]
    acc = None
    k0 = 0
    for a_ref in a_refs:
        kk = a_ref.shape[1]
        part = _dot(a_ref[...], w_ref[k0:k0 + kk, :])
        acc = part if acc is None else acc + part
        k0 += kk
    o_ref[...] = x_ref[...] + mod_ref[0, gt_idx:gt_idx + 1, :] * acc


def _mm_res(a_list, w, x, mod, gt_idx, seq, tm=512, tn=2048):
    n, d = x.shape
    tpb = seq // tm
    in_specs = [pl.BlockSpec((tm, a.shape[1]), lambda i, j: (i, 0)) for a in a_list]
    in_specs += [pl.BlockSpec((w.shape[0], tn), lambda i, j: (0, j)),
                 pl.BlockSpec((tm, tn), lambda i, j: (i, j)),
                 pl.BlockSpec((1, 6, tn), lambda i, j: (i // tpb, 0, j))]
    return pl.pallas_call(
        functools.partial(_mm_res_body, n_a=len(a_list), gt_idx=gt_idx),
        grid=(n // tm, d // tn),
        in_specs=in_specs,
        out_specs=pl.BlockSpec((tm, tn), lambda i, j: (i, j)),
        out_shape=jax.ShapeDtypeStruct((n, d), F32),
        compiler_params=_cparams(("arbitrary", "arbitrary")),
        name="proj_out_res",
    )(*a_list, w, x, mod)


def _chunk_prefix(g):
    r = g.shape[0]
    sub_rows = 8
    per_chunk = HG_CHUNK // sub_rows
    x = g.reshape(r // sub_rows, sub_rows, HG_D)
    sub = lax.broadcasted_iota(I32, x.shape, 1)
    for s in (1, 2, 4):
        x = x + jnp.where(sub >= s, pltpu.roll(x, s, 1), 0.0)
    tot = x[:, sub_rows - 1:sub_rows, :].reshape(r // HG_CHUNK, per_chunk, 1, HG_D)
    c1 = tot[:, 0:1]
    c2 = c1 + tot[:, 1:2]
    c3 = c2 + tot[:, 2:3]
    carry = jnp.concatenate([jnp.zeros_like(c1), c1, c2, c3], axis=1)
    return x.reshape(r // HG_CHUNK, per_chunk, sub_rows, HG_D) + carry, c3 + tot[:, 3:4]


def _hgrn_body(q_ref, i_ref, ff_ref, fb_ref, gg_ref, lbl_ref, ng_ref, o_ref,
               qg_s, kd_s, dec_s, vt_s, oacc_s, *, seq):
    blk = HG_BLK
    sblk = HG_SBLK
    cpb = blk // HG_CHUNK
    cps = sblk // HG_CHUNK
    lg = lbl_ref[...]
    ex = jnp.exp(lg - jnp.max(lg, axis=0, keepdims=True))
    lb = ex[0:1, :] / jnp.sum(ex, axis=0, keepdims=True)

    row = lax.broadcasted_iota(I32, (blk, blk), 0)
    col = lax.broadcasted_iota(I32, (blk, blk), 1)
    same_chunk = (row // HG_CHUNK) == (col // HG_CHUNK)
    masks = (same_chunk & (row >= col), same_chunk & (row <= col))
    srow = lax.broadcasted_iota(I32, (sblk, HG_D), 0) // HG_CHUNK
    f_refs = (ff_ref, fb_ref)
    shape4 = (cpb, HG_CHUNK // 8, 8, HG_D)
    lanes = [slice(hd * HG_D, (hd + 1) * HG_D) for hd in range(HG_HEADS_PER_STEP)]

    def stage1(b, carry):
        sl = pl.ds(pl.multiple_of(b * blk, blk), blk)
        for hd in range(HG_HEADS_PER_STEP):
            ln = lanes[hd]
            qh = _silu(q_ref[0, sl, ln].astype(F32)).reshape(shape4)
            v = i_ref[0, sl, ln]
            vf = v.astype(F32)
            for half in range(blk // sblk):
                vt_s[hd, b * (blk // sblk) + half] = vf[half * sblk:(half + 1) * sblk, :].T.astype(BF16)
            o_blk = None
            for d in range(2):
                f = lb[:, ln] + (1.0 - lb[:, ln]) * jax.nn.sigmoid(f_refs[d][0, sl, ln].astype(F32))
                g = jnp.log(f)
                k = (1.0 - f).reshape(shape4)
                pref, total = _chunk_prefix(g)
                gc = pref if d == 0 else total - pref + g.reshape(shape4)
                eg = jnp.exp(gc)
                dec = jnp.exp(total)
                qg = (qh * eg).reshape(blk, HG_D).astype(BF16)
                kgf = k * (1.0 / eg)
                kg = kgf.reshape(blk, HG_D).astype(BF16)
                qg_s[hd, d, sl, :] = qg
                kd_s[hd, d, sl, :] = (kgf * dec).reshape(blk, HG_D).astype(BF16)
                dec_s[hd, d, pl.ds(pl.multiple_of(b * cpb, cpb), cpb)] = dec.reshape(cpb, 1, HG_D)
                a = jnp.where(masks[d], _dot_nt(qg, kg), 0.0).astype(BF16)
                o = _dot(a, v)
                o_blk = o if o_blk is None else o_blk + o
            oacc_s[hd, sl, :] = o_blk
        return carry

    lax.fori_loop(0, seq // blk, stage1, 0)

    n_sblk = seq // sblk
    chains = [(hd, d) for hd in range(HG_HEADS_PER_STEP) for d in range(2)]

    def stage2(j, carry):
        states = list(carry)
        sbs = (j, n_sblk - 1 - j)
        uts = []
        for hd, d in chains:
            vt_b = vt_s[hd, sbs[d]]
            kd_b = kd_s[hd, d, pl.ds(pl.multiple_of(sbs[d] * sblk, sblk), sblk), :]
            uts.append([_dot(vt_b, jnp.where(srow == jj, kd_b, jnp.zeros_like(kd_b))) for jj in range(cps)])
        inter = [[None] * cps for _ in chains]
        for step in range(cps):
            for c, (hd, d) in enumerate(chains):
                jj = step if d == 0 else cps - 1 - step
                rows = pl.ds(pl.multiple_of(sbs[d] * sblk + jj * HG_CHUNK, HG_CHUNK), HG_CHUNK)
                inter[c][jj] = _dot_nt(qg_s[hd, d, rows, :], states[c].astype(BF16))
                states[c] = states[c] * dec_s[hd, d, sbs[d] * cps + jj] + uts[c][jj]
        for c, (hd, d) in enumerate(chains):
            rows = pl.ds(pl.multiple_of(sbs[d] * sblk, sblk), sblk)
            oacc_s[hd, rows, :] += jnp.concatenate(inter[c], axis=0)
        return tuple(states)

    zero = jnp.zeros((HG_D, HG_D), F32)
    lax.fori_loop(0, n_sblk, stage2, (zero,) * len(chains))

    for hd in range(HG_HEADS_PER_STEP):
        y = _rms(oacc_s[hd]) * ng_ref[hd]
        o_ref[0, :, lanes[hd]] = (y * _silu(gg_ref[0, :, lanes[hd]].astype(F32))).astype(o_ref.dtype)


def _hgrn(proj, lb_logits, norm_g):
    bsz, seq, _ = proj.shape
    n_chunks = seq // HG_CHUNK
    n_layers = lb_logits.shape[0]

    hps = HG_HEADS_PER_STEP
    steps = HG_HEADS // hps
    width = hps * HG_D

    def seg_spec(seg):
        return pl.BlockSpec((1, seq, width), lambda b, h: (b, 0, seg * steps + h))

    return pl.pallas_call(
        functools.partial(_hgrn_body, seq=seq),
        grid=(bsz, steps),
        in_specs=[seg_spec(0), seg_spec(1), seg_spec(2), seg_spec(3), seg_spec(4),
                  pl.BlockSpec((n_layers, width), lambda b, h: (0, h)),
                  pl.BlockSpec((hps, 1, HG_D), lambda b, h: (h, 0, 0))],
        out_specs=pl.BlockSpec((1, seq, width), lambda b, h: (b, 0, h)),
        out_shape=jax.ShapeDtypeStruct((bsz, seq, HG_HEADS * HG_D), BF16),
        scratch_shapes=[pltpu.VMEM((hps, 2, seq, HG_D), BF16), pltpu.VMEM((hps, 2, seq, HG_D), BF16),
                        pltpu.VMEM((hps, 2, n_chunks, 1, HG_D), F32),
                        pltpu.VMEM((hps, seq // HG_SBLK, HG_D, HG_SBLK), BF16),
                        pltpu.VMEM((hps, seq, HG_D), F32)],
        compiler_params=_cparams(("arbitrary", "arbitrary")),
        name="hgrn2",
    )(proj, proj, proj, proj, proj, lb_logits, norm_g.reshape(HG_HEADS, 1, HG_D))


def _na_bias_table(rpb):
    col = jnp.arange(GRID_W)
    dc = jnp.clip(col[None, :] - col[:, None] + (NA_KW - 1), 0, 2 * NA_KW - 2)
    sel = (dc[None] == jnp.arange(2 * NA_KW - 1)[:, None, None]).astype(F32)
    toe = jnp.einsum("hrd,dqk->hrqk", rpb.astype(F32), sel, precision=lax.Precision.HIGHEST)
    pats = []
    for p in range(NA_KH):
        w = toe[:, NA_KH - 1 - p:2 * NA_KH - 1 - p]
        pats.append(w.transpose(0, 2, 1, 3).reshape(rpb.shape[0], GRID_W, NA_KH * GRID_W))
    return jnp.stack(pats, axis=1)


def _na_body(q_ref, k_ref, v_ref, bt_ref, o_ref, *, seq):
    rows = seq // GRID_W
    win = NA_KH * GRID_W
    cq = lax.broadcasted_iota(I32, (GRID_W, win), 0)
    ck = lax.broadcasted_iota(I32, (GRID_W, win), 1) % GRID_W
    cs = jnp.clip(cq - NA_KW // 2, 0, GRID_W - NA_KW)
    in_win = (ck >= cs) & (ck < cs + NA_KW)
    scale = NA_DH ** -0.5
    heads_here = LANES // NA_DH
    lane_head = lax.broadcasted_iota(I32, (GRID_W, LANES), 1) // NA_DH
    out_head = lane_head

    def row_group(rg, carry):
        items = []
        for u in range(NA_ROWS_PER_STEP):
            r = rg * NA_ROWS_PER_STEP + u
            start = jnp.clip(r - NA_KH // 2, 0, rows - NA_KH)
            qs = pl.ds(pl.multiple_of(r * GRID_W, GRID_W), GRID_W)
            ks = pl.ds(pl.multiple_of(start * GRID_W, GRID_W), win)
            q = q_ref[0, qs, :]
            k = k_ref[0, ks, :]
            for hh in range(heads_here):
                qh = jnp.where(lane_head == hh, q, jnp.zeros_like(q))
                items.append((_dot_nt(qh, k), hh, r - start, qs, ks))
        probs = []
        for s, hh, pat, qs, ks in items:
            s = jnp.where(in_win, s * scale + bt_ref[hh, pat], NEG_BIG)
            e = jnp.exp(s - jnp.max(s, axis=-1, keepdims=True))
            probs.append((e.astype(BF16), 1.0 / jnp.sum(e, axis=-1, keepdims=True), qs, ks))
        for i in range(0, len(probs), heads_here):
            qs, ks = probs[i][2], probs[i][3]
            v = v_ref[0, ks, :]
            o = None
            for hh in range(heads_here):
                oh = _dot(probs[i + hh][0], v) * probs[i + hh][1]
                o = oh if o is None else jnp.where(out_head == hh, oh, o)
            o_ref[0, qs, :] = o.astype(o_ref.dtype)
        return carry

    lax.fori_loop(0, rows // NA_ROWS_PER_STEP, row_group, 0)


def _na(proj, rpb, col0):
    bsz, seq, _ = proj.shape
    hpb = LANES // NA_DH
    nblk = NA_HEADS // hpb
    b0 = col0 // LANES
    bt = _na_bias_table(rpb)

    def seg_spec(seg):
        return pl.BlockSpec((1, seq, LANES), lambda b, h: (b, 0, b0 + seg * nblk + h))

    return pl.pallas_call(
        functools.partial(_na_body, seq=seq),
        grid=(bsz, nblk),
        in_specs=[seg_spec(0), seg_spec(1), seg_spec(2),
                  pl.BlockSpec((hpb, NA_KH, GRID_W, NA_KH * GRID_W), lambda b, h: (h, 0, 0, 0))],
        out_specs=pl.BlockSpec((1, seq, LANES), lambda b, h: (b, 0, h)),
        out_shape=jax.ShapeDtypeStruct((bsz, seq, NA_HEADS * NA_DH), BF16),
        compiler_params=_cparams(("arbitrary", "arbitrary")),
        name="natten",
    )(proj, proj, proj, bt)


def _da_body(q_ref, k_ref, v_ref, lam_ref, g_ref, o_ref, *, lam_init):
    lp = lam_ref[...]
    lam = (jnp.exp(jnp.sum(lp[0:1] * lp[1:2], axis=-1, keepdims=True))
           - jnp.exp(jnp.sum(lp[2:3] * lp[3:4], axis=-1, keepdims=True)) + lam_init)
    scale2 = (DA_DH ** -0.5) * math.log2(math.e)
    v = v_ref[0]

    def unnormalised(s):
        e = jnp.exp2(s * scale2 - jnp.max(s, axis=-1, keepdims=True) * scale2)
        return e, 1.0 / jnp.sum(e, axis=-1, keepdims=True)

    rows_per = q_ref.shape[1] // DA_ROW_PIECES
    pieces = [slice(p * rows_per, (p + 1) * rows_per) for p in range(DA_ROW_PIECES)]
    ss = [[_dot_nt(q_ref[0, rs, lo:lo + DA_DH], k_ref[0, :, lo:lo + DA_DH]) for lo in (0, DA_DH)]
          for rs in pieces]
    for rs, (s1, s2) in zip(pieces, ss):
        e1, r1 = unnormalised(s1)
        e2, r2 = unnormalised(s2)
        o = _dot((e1 * r1 - e2 * (lam * r2)).astype(BF16), v)
        o_ref[0, rs, :] = (_rms(o) * g_ref[...] * (1.0 - lam_init)).astype(o_ref.dtype)


def _da(proj, diff_lambda, diff_norm, layer_idx, tq=1024):
    bsz, seq, _ = proj.shape
    tq = min(tq, seq)
    dv = 2 * DA_DH
    lam_init = 0.8 - 0.6 * math.exp(-0.3 * layer_idx)
    return pl.pallas_call(
        functools.partial(_da_body, lam_init=lam_init),
        grid=(bsz, DA_HEADS, seq // tq),
        in_specs=[pl.BlockSpec((1, tq, dv), lambda b, h, i: (b, i, h)),
                  pl.BlockSpec((1, seq, dv), lambda b, h, i: (b, 0, DA_HEADS + h)),
                  pl.BlockSpec((1, seq, dv), lambda b, h, i: (b, 0, 2 * DA_HEADS + h)),
                  pl.BlockSpec((4, DA_DH), lambda b, h, i: (0, 0)),
                  pl.BlockSpec((1, dv), lambda b, h, i: (0, 0))],
        out_specs=pl.BlockSpec((1, tq, dv), lambda b, h, i: (b, i, h)),
        out_shape=jax.ShapeDtypeStruct((bsz, seq, DA_HEADS * dv), BF16),
        compiler_params=_cparams(("arbitrary", "arbitrary", "arbitrary")),
        name="diff_attn",
    )(proj, proj, proj, diff_lambda, diff_norm.reshape(1, dv))


def _route_body(x_ref, g_ref, mod_ref, rwc_ref, rwh_ref, rb_ref,
                h_ref, e_ref, gate_ref, rank_ref, cnt_ref, run_s, *, tm):
    i = pl.program_id(0)

    @pl.when(i == 0)
    def _():
        run_s[...] = jnp.zeros_like(run_s)

    h = _norm_mod(x_ref[...], g_ref[...], mod_ref[0, 3:4, :], mod_ref[0, 4:5, :])
    h_ref[...] = h
    h_hi = h.astype(BF16)
    h_lo = (h - h_hi.astype(F32)).astype(BF16)
    both = _dot_nt(rwc_ref[...], h_hi)
    logits = both[0:N_EXPERTS] + both[N_EXPERTS:2 * N_EXPERTS] + _dot_nt(rwh_ref[...], h_lo)
    scores = jax.nn.sigmoid(logits)
    sel = scores + rb_ref[...]

    sc = [scores[e:e + 1, :] for e in range(N_EXPERTS)]
    sl = [sel[e:e + 1, :] for e in range(N_EXPERTS)]

    def top2_sum(a, b, c, d):
        hi1, lo1 = jnp.maximum(a, b), jnp.minimum(a, b)
        hi2, lo2 = jnp.maximum(c, d), jnp.minimum(c, d)
        return jnp.maximum(hi1, hi2) + jnp.maximum(jnp.minimum(hi1, hi2), jnp.maximum(lo1, lo2))

    gs = [top2_sum(*sl[4 * g:4 * g + 4]) for g in range(N_GROUPS)]
    best = jnp.zeros_like(gs[0], dtype=I32)
    best_v = gs[0]
    for g in range(1, N_GROUPS):
        better = gs[g] > best_v
        best = jnp.where(better, g, best)
        best_v = jnp.where(better, gs[g], best_v)

    def pick(vals, j):
        out = vals[j]
        for g in range(1, N_GROUPS):
            out = jnp.where(best == g, vals[4 * g + j], out)
        return out

    v = [pick(sl, j) for j in range(EXPERTS_PER_GROUP)]
    w = [pick(sc, j) for j in range(EXPERTS_PER_GROUP)]

    def argmax4(vals, excluded):
        idx = jnp.full_like(best, -1)
        cur = jnp.full_like(vals[0], -jnp.inf)
        for j in range(EXPERTS_PER_GROUP):
            ok = (vals[j] > cur) if excluded is None else ((vals[j] > cur) & (excluded != j))
            idx = jnp.where(ok, j, idx)
            cur = jnp.where(ok, vals[j], cur)
        return idx

    i1 = argmax4(v, None)
    i2 = argmax4(v, i1)

    def take(vals, idx):
        out = vals[0]
        for j in range(1, EXPERTS_PER_GROUP):
            out = jnp.where(idx == j, vals[j], out)
        return out

    w1, w2 = take(w, i1), take(w, i2)
    den = w1 + w2
    e1 = best * EXPERTS_PER_GROUP + i1
    e2 = best * EXPERTS_PER_GROUP + i2
    e_ref[0:1, :] = e1
    e_ref[1:2, :] = e2
    gate_ref[0:1, :] = w1 / den
    gate_ref[1:2, :] = w2 / den

    eid = lax.broadcasted_iota(I32, (N_EXPERTS, tm), 0)
    oh1 = eid == e1
    oh2 = eid == e2
    onehot = (oh1 | oh2).astype(BF16)
    srow = lax.broadcasted_iota(I32, (tm, tm), 0)
    tcol = lax.broadcasted_iota(I32, (tm, tm), 1)
    before = (srow < tcol).astype(BF16)
    pos = _dot(onehot, before) + run_s[...]
    rank_ref[0:1, :] = jnp.sum(jnp.where(oh1, pos, 0.0), axis=0, keepdims=True).astype(I32)
    rank_ref[1:2, :] = jnp.sum(jnp.where(oh2, pos, 0.0), axis=0, keepdims=True).astype(I32)
    run_s[...] = run_s[...] + jnp.sum(onehot.astype(F32), axis=1, keepdims=True)
    cnt_ref[...] = jnp.broadcast_to(run_s[...], cnt_ref.shape).astype(I32)


def _route(x, g, mod, router_w, router_b, seq, tm=512):
    n, d = x.shape
    tpb = seq // tm
    rwt = router_w.T
    rw_hi = rwt.astype(BF16)
    rw_lo = (rwt - rw_hi.astype(F32)).astype(BF16)
    rwc = jnp.concatenate([rw_hi, rw_lo], axis=0)
    pair_spec = pl.BlockSpec((2, tm), lambda i: (0, i))
    return pl.pallas_call(
        functools.partial(_route_body, tm=tm),
        grid=(n // tm,),
        in_specs=[pl.BlockSpec((tm, d), lambda i: (i, 0)),
                  pl.BlockSpec((1, d), lambda i: (0, 0)),
                  pl.BlockSpec((1, 6, d), lambda i: (i // tpb, 0, 0)),
                  pl.BlockSpec((2 * N_EXPERTS, d), lambda i: (0, 0)),
                  pl.BlockSpec((N_EXPERTS, d), lambda i: (0, 0)),
                  pl.BlockSpec((N_EXPERTS, 1), lambda i: (0, 0))],
        out_specs=[pl.BlockSpec((tm, d), lambda i: (i, 0)), pair_spec, pair_spec, pair_spec,
                   pl.BlockSpec((N_EXPERTS, LANES), lambda i: (0, 0))],
        out_shape=[jax.ShapeDtypeStruct((n, d), F32),
                   jax.ShapeDtypeStruct((2, n), I32),
                   jax.ShapeDtypeStruct((2, n), F32),
                   jax.ShapeDtypeStruct((2, n), I32),
                   jax.ShapeDtypeStruct((N_EXPERTS, LANES), I32)],
        scratch_shapes=[pltpu.VMEM((N_EXPERTS, 1), F32)],
        compiler_params=_cparams(("arbitrary",)),
        name="moe_route",
    )(x, g.reshape(1, d), mod, rwc, rw_hi, router_b.reshape(N_EXPERTS, 1).astype(F32))


def _row_copy(src_ref, s, dst_ref, d, sem):
    return pltpu.make_async_copy(src_ref.at[pl.ds(s, 1), :], dst_ref.at[pl.ds(d, 1), :], sem)


def _dispatch_body(meta_ref, dest_ref, h_ref, xs_ref, zero_s, sem, *, tmd, tm, max_tiles):
    i = pl.program_id(0)
    zr = zero_s.shape[0]

    def zero_tile(start):
        copies = [pltpu.make_async_copy(zero_s, xs_ref.at[pl.ds(pl.multiple_of(start + z * zr, zr), zr), :], sem)
                  for z in range(tm // zr)]
        for cp in copies:
            cp.start()
        for cp in copies:
            cp.wait()

    @pl.when(i == 0)
    def _():
        zero_s[...] = jnp.zeros_like(zero_s)
        for e in range(N_EXPERTS):
            @pl.when(meta_ref[N_EXPERTS + e] > 0)
            def _():
                zero_tile(pl.multiple_of(meta_ref[e] - tm, tm))

        def fill_unused(t, carry):
            zero_tile(pl.multiple_of(t * tm, tm))
            return carry

        lax.fori_loop(meta_ref[2 * N_EXPERTS], max_tiles, fill_unused, 0)

    def issue(t, carry):
        _row_copy(h_ref, t, xs_ref, dest_ref[0, 0, t], sem).start(priority=0)
        _row_copy(h_ref, t, xs_ref, dest_ref[0, 0, tmd + t], sem).start(priority=1)
        return carry

    lax.fori_loop(0, tmd, issue, 0, unroll=4)

    def drain(t, carry):
        _row_copy(h_ref, 0, xs_ref, 0, sem).wait()
        _row_copy(h_ref, 0, xs_ref, 0, sem).wait()
        return carry

    lax.fori_loop(0, tmd, drain, 0)


def _dispatch(h, dest, meta, n_slots, tm, tmd=256):
    n, d = h.shape
    dest_blk = jnp.concatenate([dest[0].reshape(n // tmd, 1, tmd), dest[1].reshape(n // tmd, 1, tmd)], axis=2)
    grid_spec = pltpu.PrefetchScalarGridSpec(
        num_scalar_prefetch=1,
        grid=(n // tmd,),
        in_specs=[pl.BlockSpec((1, 1, 2 * tmd), lambda i, m: (i, 0, 0), memory_space=pltpu.SMEM),
                  pl.BlockSpec((tmd, d), lambda i, m: (i, 0))],
        out_specs=pl.BlockSpec(memory_space=pl.ANY),
        scratch_shapes=[pltpu.VMEM((ZERO_ROWS, d), F32), pltpu.SemaphoreType.DMA(())],
    )
    return pl.pallas_call(
        functools.partial(_dispatch_body, tmd=tmd, tm=tm, max_tiles=n_slots // tm),
        grid_spec=grid_spec,
        out_shape=jax.ShapeDtypeStruct((n_slots, d), F32),
        compiler_params=_cparams(("arbitrary",)),
        name="moe_dispatch",
    )(meta, dest_blk, h)


def _moe_up_body(te_ref, nv_ref, xs_ref, wg_ref, wu_ref, hid_ref, wgb_s, wub_s):
    i = pl.program_id(1)
    valid = i < nv_ref[0]
    changed = (i == 0) | (te_ref[i] != te_ref[jnp.maximum(i - 1, 0)])

    @pl.when(valid & changed)
    def _():
        wgb_s[...] = wg_ref[0].astype(BF16)
        wub_s[...] = wu_ref[0].astype(BF16)

    @pl.when(valid)
    def _():
        x = xs_ref[...].astype(BF16)
        a = _dot(x, wgb_s[...])
        u = _dot(x, wub_s[...])
        hid_ref[...] = (_silu(a) * u).astype(hid_ref.dtype)

    @pl.when(jnp.logical_not(valid))
    def _():
        hid_ref[...] = jnp.zeros_like(hid_ref)


def _moe_up(xs, wg, wu, te, nv, tm, fc=512):
    n_slots, d = xs.shape
    ff = wg.shape[2]
    max_tiles = n_slots // tm

    def row_map(j, i, te_r, nv_r):
        return (jnp.minimum(i, nv_r[0] - 1), 0)

    def w_map(j, i, te_r, nv_r):
        return (te_r[i], 0, j)

    grid_spec = pltpu.PrefetchScalarGridSpec(
        num_scalar_prefetch=2,
        grid=(ff // fc, max_tiles),
        in_specs=[pl.BlockSpec((tm, d), row_map),
                  pl.BlockSpec((1, d, fc), w_map),
                  pl.BlockSpec((1, d, fc), w_map)],
        out_specs=pl.BlockSpec((tm, fc), lambda j, i, te_r, nv_r: (i, j)),
        scratch_shapes=[pltpu.VMEM((d, fc), BF16), pltpu.VMEM((d, fc), BF16)],
    )
    return pl.pallas_call(
        _moe_up_body,
        grid_spec=grid_spec,
        out_shape=jax.ShapeDtypeStruct((n_slots, ff), BF16),
        compiler_params=_cparams(("arbitrary", "arbitrary")),
        name="moe_up",
    )(te, nv, xs, wg, wu)


def _moe_down_body(te_ref, nv_ref, hid_ref, wd_ref, o_ref, wdb_s):
    i = pl.program_id(1)
    valid = i < nv_ref[0]
    changed = (i == 0) | (te_ref[i] != te_ref[jnp.maximum(i - 1, 0)])

    @pl.when(valid & changed)
    def _():
        wdb_s[...] = wd_ref[0].astype(BF16)

    @pl.when(valid)
    def _():
        o_ref[...] = _dot(hid_ref[...], wdb_s[...])

    @pl.when(jnp.logical_not(valid))
    def _():
        o_ref[...] = jnp.zeros_like(o_ref)


def _moe_down(hid, wd, te, nv, tm, nc=2048):
    n_slots, ff = hid.shape
    d = wd.shape[2]
    max_tiles = n_slots // tm
    grid_spec = pltpu.PrefetchScalarGridSpec(
        num_scalar_prefetch=2,
        grid=(d // nc, max_tiles),
        in_specs=[pl.BlockSpec((tm, ff), lambda j, i, te_r, nv_r: (jnp.minimum(i, nv_r[0] - 1), 0)),
                  pl.BlockSpec((1, ff, nc), lambda j, i, te_r, nv_r: (te_r[i], 0, j))],
        out_specs=pl.BlockSpec((tm, nc), lambda j, i, te_r, nv_r: (i, j)),
        scratch_shapes=[pltpu.VMEM((ff, nc), BF16)],
    )
    return pl.pallas_call(
        _moe_down_body,
        grid_spec=grid_spec,
        out_shape=jax.ShapeDtypeStruct((n_slots, d), F32),
        compiler_params=_cparams(("arbitrary", "arbitrary")),
        name="moe_down",
    )(te, nv, hid, wd)


def _combine_body(dest_ref, x_ref, gate_ref, mod_ref, ys_ref, *rest, tc, final, with_next):
    hn_ref = None
    if final:
        lnf_ref, o_ref, buf_s, sem = rest
    elif with_next:
        lng_ref, modn_ref, o_ref, hn_ref, buf_s, sem = rest
    else:
        o_ref, buf_s, sem = rest

    i = pl.program_id(0)
    n_tiles = pl.num_programs(0) - 1
    slot = i % 2

    def row_in(s, k, t, d):
        return pltpu.make_async_copy(ys_ref.at[pl.ds(d, 1), :], buf_s.at[s, k, pl.ds(t, 1), :], sem.at[s])

    @pl.when(i < n_tiles)
    def _():
        def issue(t, carry):
            row_in(slot, 0, t, dest_ref[0, 0, t]).start(priority=0)
            row_in(slot, 1, t, dest_ref[0, 0, tc + t]).start(priority=1)
            return carry

        lax.fori_loop(0, tc, issue, 0, unroll=4)

    @pl.when(i > 0)
    def _():
        prev = 1 - slot

        def drain(t, carry):
            row_in(prev, 0, 0, 0).wait()
            row_in(prev, 1, 0, 0).wait()
            return carry

        lax.fori_loop(0, tc, drain, 0)

        gates = gate_ref[...]
        y = x_ref[...] + mod_ref[0, 5:6, :] * (gates[:, 0:1] * buf_s[prev, 0] + gates[:, 1:2] * buf_s[prev, 1])
        if final:
            y = _rms(y) * lnf_ref[...]
        o_ref[...] = y
        if hn_ref is not None:
            hn_ref[...] = _norm_mod(y, lng_ref[...], modn_ref[0, 0:1, :], modn_ref[0, 1:2, :]).astype(hn_ref.dtype)


def _combine(x, ys, dest, gates, mod, seq, ln_final=None, next_norm=None, tc=512):
    n, d = x.shape
    tpb = seq // tc
    final = ln_final is not None
    with_next = next_norm is not None
    n_tiles = n // tc
    dest_blk = jnp.concatenate([dest[0].reshape(n_tiles, 1, tc), dest[1].reshape(n_tiles, 1, tc)], axis=2)

    def done(i):
        return jnp.maximum(i - 1, 0)

    row_spec = pl.BlockSpec((tc, d), lambda i: (done(i), 0))
    mod_spec = pl.BlockSpec((1, 6, d), lambda i: (done(i) // tpb, 0, 0))
    vec_spec = pl.BlockSpec((1, d), lambda i: (0, 0))
    in_specs = [pl.BlockSpec((1, 1, 2 * tc), lambda i: (jnp.minimum(i, n_tiles - 1), 0, 0), memory_space=pltpu.SMEM),
                row_spec,
                pl.BlockSpec((tc, 2), lambda i: (done(i), 0)),
                mod_spec,
                pl.BlockSpec(memory_space=pl.ANY)]
    args = [dest_blk, x, gates.T, mod, ys]
    if final:
        in_specs.append(vec_spec)
        args.append(ln_final.reshape(1, d))
    out_specs, out_shape = row_spec, jax.ShapeDtypeStruct((n, d), F32)
    if with_next:
        in_specs += [vec_spec, mod_spec]
        args += [next_norm[0].reshape(1, d), next_norm[1]]
        out_specs, out_shape = [row_spec, row_spec], [out_shape, jax.ShapeDtypeStruct((n, d), BF16)]
    return pl.pallas_call(
        functools.partial(_combine_body, tc=tc, final=final, with_next=with_next),
        grid=(n_tiles + 1,),
        in_specs=in_specs,
        out_specs=out_specs,
        out_shape=out_shape,
        scratch_shapes=[pltpu.VMEM((2, 2, tc, d), F32), pltpu.SemaphoreType.DMA((2,))],
        compiler_params=_cparams(("arbitrary",)),
        name="moe_combine",
    )(*args)


def _moe(x, ln_g, mod, router_w, router_b, wg, wu, wd, seq, ln_final=None, next_norm=None, tm=512):
    n, d = x.shape
    h, eidx, gates, rank, cnt = _route(x, ln_g, mod, router_w, router_b, seq)
    counts = cnt[:, 0]
    ntile = (counts + tm - 1) // tm
    cum = jnp.cumsum(ntile)
    off = (cum - ntile) * tm
    max_tiles = (2 * n) // tm + N_EXPERTS
    nv = cum[-1:].astype(I32)
    tiles = jnp.minimum(jnp.arange(max_tiles, dtype=I32), nv[0] - 1)
    te = jnp.minimum(jnp.sum((tiles[:, None] >= cum[None, :]).astype(I32), axis=1), N_EXPERTS - 1)
    expert_ids = jnp.arange(N_EXPERTS, dtype=I32)[:, None, None]
    dest = rank + jnp.sum(jnp.where(eidx[None] == expert_ids, off[:, None, None], 0), axis=0)
    dest = dest.astype(I32)
    meta = jnp.concatenate([cum * tm, ntile, nv]).astype(I32)
    xs = _dispatch(h, dest, meta, max_tiles * tm, tm)
    hid = _moe_up(xs, wg, wu, te, nv, tm)
    ys = _moe_down(hid, wd, te, nv, tm)
    return _combine(x, ys, dest, gates, mod, seq, ln_final, next_norm)


def kernel(x, c, ln_attn_0, ada_w_0, ada_b_0, w_in_0, hgrn_norm_0, na_rpb_0, w_out_0, ln_ffn_0, moe_wg_0, moe_wu_0, moe_wd_0, ln_attn_1, ada_w_1, ada_b_1, w_in_1, diff_lambda_1, diff_norm_1, w_out_1, ln_ffn_1, moe_wg_1, moe_wu_1, moe_wd_1, hgrn_lb_logits, router_w, router_b, ln_final):
    bsz, seq, d = x.shape
    n = bsz * seq
    xf = x.reshape(n, d)
    mod0 = _ada(c, ada_w_0, ada_b_0)
    mod1 = _ada(c, ada_w_1, ada_b_1)

    h = _norm(xf, ln_attn_0, mod0, 0, 1, seq)
    proj = _mm(h, w_in_0.astype(BF16)).reshape(bsz, seq, -1)
    o_a = _hgrn(proj, hgrn_lb_logits, hgrn_norm_0)
    o_b = _na(proj, na_rpb_0, 5 * HG_HEADS * HG_D)
    xf = _mm_res([o_a.reshape(n, -1), o_b.reshape(n, -1)], w_out_0.astype(BF16), xf, mod0, 2, seq)
    xf, h = _moe(xf, ln_ffn_0, mod0, router_w, router_b, moe_wg_0, moe_wu_0, moe_wd_0, seq,
                 next_norm=(ln_attn_1, mod1))

    proj = _mm_rope(h, w_in_1.astype(BF16), seq, 4 * DA_HEADS * DA_DH).reshape(bsz, seq, -1)
    o_c = _da(proj, diff_lambda_1, diff_norm_1, 1)
    xf = _mm_res([o_c.reshape(n, -1)], w_out_1.astype(BF16), xf, mod1, 2, seq)
    xf = _moe(xf, ln_ffn_1, mod1, router_w, router_b, moe_wg_1, moe_wu_1, moe_wd_1, seq, ln_final=ln_final)
    return xf.reshape(bsz, seq, d)
```

```python
import functools
import math

import jax
import jax.numpy as jnp
from jax import lax
from jax.experimental import pallas as pl
from jax.experimental.pallas import tpu as pltpu

F32 = jnp.float32
BF16 = jnp.bfloat16
I32 = jnp.int32

EPS = 1e-6
LANES = 128
MXU_N = 256
GRID_W = 64
HG_HEADS = 8
HG_D = 128
HG_CHUNK = 32
HG_BLK = 256
HG_SBLK = 128
HG_HEADS_PER_STEP = 4
NA_HEADS = 16
NA_DH = 64
NA_KH = 8
NA_KW = 16
NA_ROWS_PER_STEP = 16
DA_HEADS = 8
DA_DH = 128
DA_ROW_PIECES = 4
ROPE_THETA = 500000.0
ROPE_DIM = DA_DH // 4
N_EXPERTS = 16
N_GROUPS = 4
EXPERTS_PER_GROUP = 4
NEG_BIG = -1e30
ZERO_ROWS = 64

VMEM_LIMIT = 56 * 1024 * 1024


def _cparams(sem):
    return pltpu.CompilerParams(dimension_semantics=sem, vmem_limit_bytes=VMEM_LIMIT)


def _dot(a, b):
    return jnp.dot(a, b, preferred_element_type=F32)


def _dot_nt(a, b):
    return lax.dot_general(a, b, (((1,), (1,)), ((), ())), preferred_element_type=F32)


def _silu(x):
    return x * jax.nn.sigmoid(x)


def _rms(x):
    return x * lax.rsqrt(jnp.mean(x * x, axis=-1, keepdims=True) + EPS)


def _ada_body(c_ref, w_ref, b_ref, o_ref):
    a = _silu(c_ref[...]).astype(BF16)
    o_ref[...] = _dot(a, w_ref[...].astype(BF16)) + b_ref[...]


def _ada(c, w, b):
    bsz, d = c.shape
    n = w.shape[1]
    rows = 16
    c16 = jnp.concatenate([c, jnp.zeros((rows - bsz, d), c.dtype)], axis=0)
    tn = 1024
    out = pl.pallas_call(
        _ada_body,
        grid=(n // tn,),
        in_specs=[pl.BlockSpec((rows, d), lambda j: (0, 0)),
                  pl.BlockSpec((d, tn), lambda j: (0, j)),
                  pl.BlockSpec((1, tn), lambda j: (0, j))],
        out_specs=pl.BlockSpec((rows, tn), lambda j: (0, j)),
        out_shape=jax.ShapeDtypeStruct((rows, n), F32),
        compiler_params=_cparams(("arbitrary",)),
        name="ada_mod",
    )(c16, w, b.reshape(1, n))
    return out[:bsz].reshape(bsz, 6, d)


def _norm_mod(x, g, sh, sc):
    return (_rms(x) * g) * (1.0 + sc) + sh


def _norm_body(x_ref, g_ref, mod_ref, o_ref, *, sh_idx, sc_idx):
    h = _norm_mod(x_ref[...], g_ref[...], mod_ref[0, sh_idx:sh_idx + 1, :], mod_ref[0, sc_idx:sc_idx + 1, :])
    o_ref[...] = h.astype(o_ref.dtype)


def _norm(x, g, mod, sh_idx, sc_idx, seq, tm=512):
    n, d = x.shape
    tpb = seq // tm
    return pl.pallas_call(
        functools.partial(_norm_body, sh_idx=sh_idx, sc_idx=sc_idx),
        grid=(n // tm,),
        in_specs=[pl.BlockSpec((tm, d), lambda i: (i, 0)),
                  pl.BlockSpec((1, d), lambda i: (0, 0)),
                  pl.BlockSpec((1, 6, d), lambda i: (i // tpb, 0, 0))],
        out_specs=pl.BlockSpec((tm, d), lambda i: (i, 0)),
        out_shape=jax.ShapeDtypeStruct((n, d), BF16),
        compiler_params=_cparams(("arbitrary",)),
        name="norm_mod",
    )(x, g.reshape(1, d), mod)


def _mm_body(a_ref, w_ref, o_ref):
    o_ref[...] = _dot(a_ref[...], w_ref[...]).astype(o_ref.dtype)


def _mm(a, w, tm=1024, tn=1024):
    n, k = a.shape
    nout = w.shape[1]
    tm = min(tm, n)
    return pl.pallas_call(
        _mm_body,
        grid=(n // tm, nout // tn),
        in_specs=[pl.BlockSpec((tm, k), lambda i, j: (i, 0)),
                  pl.BlockSpec((k, tn), lambda i, j: (0, j))],
        out_specs=pl.BlockSpec((tm, tn), lambda i, j: (i, j)),
        out_shape=jax.ShapeDtypeStruct((n, nout), BF16),
        compiler_params=_cparams(("arbitrary", "arbitrary")),
        name="proj_in",
    )(a, w)


def _mm_rope_body(a_ref, w_ref, cos_ref, sa_ref, sb_ref, o_ref, *, n_rope_tiles, tn):
    j = pl.program_id(1)

    @pl.when(j < n_rope_tiles)
    def _():
        cos, sa, sb = cos_ref[...], sa_ref[...], sb_ref[...]
        half = ROPE_DIM // 2
        a = a_ref[...]
        for c0 in range(0, tn, MXU_N):
            acc = _dot(a, w_ref[:, c0:c0 + MXU_N])
            for c in range(0, MXU_N, LANES):
                seg = acc[:, c:c + LANES]
                r = seg * cos + pltpu.roll(seg, LANES - half, 1) * sa + pltpu.roll(seg, half, 1) * sb
                o_ref[:, c0 + c:c0 + c + LANES] = r.astype(o_ref.dtype)

    @pl.when(j >= n_rope_tiles)
    def _():
        o_ref[...] = _dot(a_ref[...], w_ref[...]).astype(o_ref.dtype)


def _rope_tables(seq):
    half = ROPE_DIM // 2
    inv = ROPE_THETA ** (-jnp.arange(half, dtype=F32) / half)
    ang = jnp.arange(seq, dtype=F32)[:, None] * inv[None, :]
    cos, sin = jnp.cos(ang), jnp.sin(ang)
    pad = jnp.zeros((seq, LANES - ROPE_DIM), F32)
    z = jnp.zeros((seq, half), F32)
    cos_t = jnp.concatenate([cos, cos, pad + 1.0], axis=1)
    sa_t = jnp.concatenate([-sin, z, pad], axis=1)
    sb_t = jnp.concatenate([z, sin, pad], axis=1)
    return cos_t, sa_t, sb_t


def _mm_rope(a, w, seq, rope_cols, tm=1024, tn=1024):
    n, k = a.shape
    nout = w.shape[1]
    tm = min(tm, seq)
    tpb = seq // tm
    cos_t, sa_t, sb_t = _rope_tables(seq)
    tab_spec = pl.BlockSpec((tm, LANES), lambda i, j: (i % tpb, 0))
    return pl.pallas_call(
        functools.partial(_mm_rope_body, n_rope_tiles=rope_cols // tn, tn=tn),
        grid=(n // tm, nout // tn),
        in_specs=[pl.BlockSpec((tm, k), lambda i, j: (i, 0)),
                  pl.BlockSpec((k, tn), lambda i, j: (0, j)),
                  tab_spec, tab_spec, tab_spec],
        out_specs=pl.BlockSpec((tm, tn), lambda i, j: (i, j)),
        out_shape=jax.ShapeDtypeStruct((n, nout), BF16),
        compiler_params=_cparams(("arbitrary", "arbitrary")),
        name="proj_in_rope",
    )(a, w, cos_t, sa_t, sb_t)


def _mm_res_body(*refs, n_a, gt_idx):
    a_refs = refs[:n_a]
    w_ref, x_ref, mod_ref, o_ref = refs[n_a:]
    acc = None
    k0 = 0
    for a_ref in a_refs:
        kk = a_ref.shape[1]
        part = _dot(a_ref[...], w_ref[k0:k0 + kk, :])
        acc = part if acc is None else acc + part
        k0 += kk
    o_ref[...] = x_ref[...] + mod_ref[0, gt_idx:gt_idx + 1, :] * acc


def _mm_res(a_list, w, x, mod, gt_idx, seq, tm=512, tn=2048):
    n, d = x.shape
    tpb = seq // tm
    in_specs = [pl.BlockSpec((tm, a.shape[1]), lambda i, j: (i, 0)) for a in a_list]
    in_specs += [pl.BlockSpec((w.shape[0], tn), lambda i, j: (0, j)),
                 pl.BlockSpec((tm, tn), lambda i, j: (i, j)),
                 pl.BlockSpec((1, 6, tn), lambda i, j: (i // tpb, 0, j))]
    return pl.pallas_call(
        functools.partial(_mm_res_body, n_a=len(a_list), gt_idx=gt_idx),
        grid=(n // tm, d // tn),
        in_specs=in_specs,
        out_specs=pl.BlockSpec((tm, tn), lambda i, j: (i, j)),
        out_shape=jax.ShapeDtypeStruct((n, d), F32),
        compiler_params=_cparams(("arbitrary", "arbitrary")),
        name="proj_out_res",
    )(*a_list, w, x, mod)


def _chunk_prefix(g):
    r = g.shape[0]
    sub_rows = 8
    per_chunk = HG_CHUNK // sub_rows
    x = g.reshape(r // sub_rows, sub_rows, HG_D)
    sub = lax.broadcasted_iota(I32, x.shape, 1)
    for s in (1, 2, 4):
        x = x + jnp.where(sub >= s, pltpu.roll(x, s, 1), 0.0)
    tot = x[:, sub_rows - 1:sub_rows, :].reshape(r // HG_CHUNK, per_chunk, 1, HG_D)
    c1 = tot[:, 0:1]
    c2 = c1 + tot[:, 1:2]
    c3 = c2 + tot[:, 2:3]
    carry = jnp.concatenate([jnp.zeros_like(c1), c1, c2, c3], axis=1)
    return x.reshape(r // HG_CHUNK, per_chunk, sub_rows, HG_D) + carry, c3 + tot[:, 3:4]


def _hgrn_body(q_ref, i_ref, ff_ref, fb_ref, gg_ref, lbl_ref, ng_ref, o_ref,
               qg_s, kd_s, dec_s, vt_s, oacc_s, *, seq):
    blk = HG_BLK
    sblk = HG_SBLK
    cpb = blk // HG_CHUNK
    cps = sblk // HG_CHUNK
    lg = lbl_ref[...]
    ex = jnp.exp(lg - jnp.max(lg, axis=0, keepdims=True))
    lb = ex[0:1, :] / jnp.sum(ex, axis=0, keepdims=True)

    row = lax.broadcasted_iota(I32, (blk, blk), 0)
    col = lax.broadcasted_iota(I32, (blk, blk), 1)
    same_chunk = (row // HG_CHUNK) == (col // HG_CHUNK)
    masks = (same_chunk & (row >= col), same_chunk & (row <= col))
    srow = lax.broadcasted_iota(I32, (sblk, HG_D), 0) // HG_CHUNK
    f_refs = (ff_ref, fb_ref)
    shape4 = (cpb, HG_CHUNK // 8, 8, HG_D)
    lanes = [slice(hd * HG_D, (hd + 1) * HG_D) for hd in range(HG_HEADS_PER_STEP)]

    def stage1(b, carry):
        sl = pl.ds(pl.multiple_of(b * blk, blk), blk)
        for hd in range(HG_HEADS_PER_STEP):
            ln = lanes[hd]
            qh = _silu(q_ref[0, sl, ln].astype(F32)).reshape(shape4)
            v = i_ref[0, sl, ln]
            vf = v.astype(F32)
            for half in range(blk // sblk):
                vt_s[hd, b * (blk // sblk) + half] = vf[half * sblk:(half + 1) * sblk, :].T.astype(BF16)
            o_blk = None
            for d in range(2):
                f = lb[:, ln] + (1.0 - lb[:, ln]) * jax.nn.sigmoid(f_refs[d][0, sl, ln].astype(F32))
                g = jnp.log(f)
                k = (1.0 - f).reshape(shape4)
                pref, total = _chunk_prefix(g)
                gc = pref if d == 0 else total - pref + g.reshape(shape4)
                eg = jnp.exp(gc)
                dec = jnp.exp(total)
                qg = (qh * eg).reshape(blk, HG_D).astype(BF16)
                kgf = k * (1.0 / eg)
                kg = kgf.reshape(blk, HG_D).astype(BF16)
                qg_s[hd, d, sl, :] = qg
                kd_s[hd, d, sl, :] = (kgf * dec).reshape(blk, HG_D).astype(BF16)
                dec_s[hd, d, pl.ds(pl.multiple_of(b * cpb, cpb), cpb)] = dec.reshape(cpb, 1, HG_D)
                a = jnp.where(masks[d], _dot_nt(qg, kg), 0.0).astype(BF16)
                o = _dot(a, v)
                o_blk = o if o_blk is None else o_blk + o
            oacc_s[hd, sl, :] = o_blk
        return carry

    lax.fori_loop(0, seq // blk, stage1, 0)

    n_sblk = seq // sblk
    chains = [(hd, d) for hd in range(HG_HEADS_PER_STEP) for d in range(2)]

    def stage2(j, carry):
        states = list(carry)
        sbs = (j, n_sblk - 1 - j)
        uts = []
        for hd, d in chains:
            vt_b = vt_s[hd, sbs[d]]
            kd_b = kd_s[hd, d, pl.ds(pl.multiple_of(sbs[d] * sblk, sblk), sblk), :]
            uts.append([_dot(vt_b, jnp.where(srow == jj, kd_b, jnp.zeros_like(kd_b))) for jj in range(cps)])
        inter = [[None] * cps for _ in chains]
        for step in range(cps):
            for c, (hd, d) in enumerate(chains):
                jj = step if d == 0 else cps - 1 - step
                rows = pl.ds(pl.multiple_of(sbs[d] * sblk + jj * HG_CHUNK, HG_CHUNK), HG_CHUNK)
                inter[c][jj] = _dot_nt(qg_s[hd, d, rows, :], states[c].astype(BF16))
                states[c] = states[c] * dec_s[hd, d, sbs[d] * cps + jj] + uts[c][jj]
        for c, (hd, d) in enumerate(chains):
            rows = pl.ds(pl.multiple_of(sbs[d] * sblk, sblk), sblk)
            oacc_s[hd, rows, :] += jnp.concatenate(inter[c], axis=0)
        return tuple(states)

    zero = jnp.zeros((HG_D, HG_D), F32)
    lax.fori_loop(0, n_sblk, stage2, (zero,) * len(chains))

    for hd in range(HG_HEADS_PER_STEP):
        y = _rms(oacc_s[hd]) * ng_ref[hd]
        o_ref[0, :, lanes[hd]] = (y * _silu(gg_ref[0, :, lanes[hd]].astype(F32))).astype(o_ref.dtype)


def _hgrn(proj, lb_logits, norm_g):
    bsz, seq, _ = proj.shape
    n_chunks = seq // HG_CHUNK
    n_layers = lb_logits.shape[0]

    hps = HG_HEADS_PER_STEP
    steps = HG_HEADS // hps
    width = hps * HG_D

    def seg_spec(seg):
        return pl.BlockSpec((1, seq, width), lambda b, h: (b, 0, seg * steps + h))

    return pl.pallas_call(
        functools.partial(_hgrn_body, seq=seq),
        grid=(bsz, steps),
        in_specs=[seg_spec(0), seg_spec(1), seg_spec(2), seg_spec(3), seg_spec(4),
                  pl.BlockSpec((n_layers, width), lambda b, h: (0, h)),
                  pl.BlockSpec((hps, 1, HG_D), lambda b, h: (h, 0, 0))],
        out_specs=pl.BlockSpec((1, seq, width), lambda b, h: (b, 0, h)),
        out_shape=jax.ShapeDtypeStruct((bsz, seq, HG_HEADS * HG_D), BF16),
        scratch_shapes=[pltpu.VMEM((hps, 2, seq, HG_D), BF16), pltpu.VMEM((hps, 2, seq, HG_D), BF16),
                        pltpu.VMEM((hps, 2, n_chunks, 1, HG_D), F32),
                        pltpu.VMEM((hps, seq // HG_SBLK, HG_D, HG_SBLK), BF16),
                        pltpu.VMEM((hps, seq, HG_D), F32)],
        compiler_params=_cparams(("arbitrary", "arbitrary")),
        name="hgrn2",
    )(proj, proj, proj, proj, proj, lb_logits, norm_g.reshape(HG_HEADS, 1, HG_D))


def _na_bias_table(rpb):
    col = jnp.arange(GRID_W)
    dc = jnp.clip(col[None, :] - col[:, None] + (NA_KW - 1), 0, 2 * NA_KW - 2)
    sel = (dc[None] == jnp.arange(2 * NA_KW - 1)[:, None, None]).astype(F32)
    toe = jnp.einsum("hrd,dqk->hrqk", rpb.astype(F32), sel, precision=lax.Precision.HIGHEST)
    pats = []
    for p in range(NA_KH):
        w = toe[:, NA_KH - 1 - p:2 * NA_KH - 1 - p]
        pats.append(w.transpose(0, 2, 1, 3).reshape(rpb.shape[0], GRID_W, NA_KH * GRID_W))
    return jnp.stack(pats, axis=1)


def _na_body(q_ref, k_ref, v_ref, bt_ref, o_ref, *, seq):
    rows = seq // GRID_W
    win = NA_KH * GRID_W
    cq = lax.broadcasted_iota(I32, (GRID_W, win), 0)
    ck = lax.broadcasted_iota(I32, (GRID_W, win), 1) % GRID_W
    cs = jnp.clip(cq - NA_KW // 2, 0, GRID_W - NA_KW)
    in_win = (ck >= cs) & (ck < cs + NA_KW)
    scale = NA_DH ** -0.5
    heads_here = LANES // NA_DH
    lane_head = lax.broadcasted_iota(I32, (GRID_W, LANES), 1) // NA_DH
    out_head = lane_head

    def row_group(rg, carry):
        items = []
        for u in range(NA_ROWS_PER_STEP):
            r = rg * NA_ROWS_PER_STEP + u
            start = jnp.clip(r - NA_KH // 2, 0, rows - NA_KH)
            qs = pl.ds(pl.multiple_of(r * GRID_W, GRID_W), GRID_W)
            ks = pl.ds(pl.multiple_of(start * GRID_W, GRID_W), win)
            q = q_ref[0, qs, :]
            k = k_ref[0, ks, :]
            for hh in range(heads_here):
                qh = jnp.where(lane_head == hh, q, jnp.zeros_like(q))
                items.append((_dot_nt(qh, k), hh, r - start, qs, ks))
        probs = []
        for s, hh, pat, qs, ks in items:
            s = jnp.where(in_win, s * scale + bt_ref[hh, pat], NEG_BIG)
            e = jnp.exp(s - jnp.max(s, axis=-1, keepdims=True))
            probs.append((e.astype(BF16), 1.0 / jnp.sum(e, axis=-1, keepdims=True), qs, ks))
        for i in range(0, len(probs), heads_here):
            qs, ks = probs[i][2], probs[i][3]
            v = v_ref[0, ks, :]
            o = None
            for hh in range(heads_here):
                oh = _dot(probs[i + hh][0], v) * probs[i + hh][1]
                o = oh if o is None else jnp.where(out_head == hh, oh, o)
            o_ref[0, qs, :] = o.astype(o_ref.dtype)
        return carry

    lax.fori_loop(0, rows // NA_ROWS_PER_STEP, row_group, 0)


def _na(proj, rpb, col0):
    bsz, seq, _ = proj.shape
    hpb = LANES // NA_DH
    nblk = NA_HEADS // hpb
    b0 = col0 // LANES
    bt = _na_bias_table(rpb)

    def seg_spec(seg):
        return pl.BlockSpec((1, seq, LANES), lambda b, h: (b, 0, b0 + seg * nblk + h))

    return pl.pallas_call(
        functools.partial(_na_body, seq=seq),
        grid=(bsz, nblk),
        in_specs=[seg_spec(0), seg_spec(1), seg_spec(2),
                  pl.BlockSpec((hpb, NA_KH, GRID_W, NA_KH * GRID_W), lambda b, h: (h, 0, 0, 0))],
        out_specs=pl.BlockSpec((1, seq, LANES), lambda b, h: (b, 0, h)),
        out_shape=jax.ShapeDtypeStruct((bsz, seq, NA_HEADS * NA_DH), BF16),
        compiler_params=_cparams(("arbitrary", "arbitrary")),
        name="natten",
    )(proj, proj, proj, bt)


def _da_body(q_ref, k_ref, v_ref, lam_ref, g_ref, o_ref, *, lam_init):
    lp = lam_ref[...]
    lam = (jnp.exp(jnp.sum(lp[0:1] * lp[1:2], axis=-1, keepdims=True))
           - jnp.exp(jnp.sum(lp[2:3] * lp[3:4], axis=-1, keepdims=True)) + lam_init)
    scale2 = (DA_DH ** -0.5) * math.log2(math.e)
    v = v_ref[0]

    def unnormalised(s):
        e = jnp.exp2(s * scale2 - jnp.max(s, axis=-1, keepdims=True) * scale2)
        return e, 1.0 / jnp.sum(e, axis=-1, keepdims=True)

    rows_per = q_ref.shape[1] // DA_ROW_PIECES
    pieces = [slice(p * rows_per, (p + 1) * rows_per) for p in range(DA_ROW_PIECES)]
    ss = [[_dot_nt(q_ref[0, rs, lo:lo + DA_DH], k_ref[0, :, lo:lo + DA_DH]) for lo in (0, DA_DH)]
          for rs in pieces]
    for rs, (s1, s2) in zip(pieces, ss):
        e1, r1 = unnormalised(s1)
        e2, r2 = unnormalised(s2)
        o = _dot((e1 * r1 - e2 * (lam * r2)).astype(BF16), v)
        o_ref[0, rs, :] = (_rms(o) * g_ref[...] * (1.0 - lam_init)).astype(o_ref.dtype)


def _da(proj, diff_lambda, diff_norm, layer_idx, tq=1024):
    bsz, seq, _ = proj.shape
    tq = min(tq, seq)
    dv = 2 * DA_DH
    lam_init = 0.8 - 0.6 * math.exp(-0.3 * layer_idx)
    return pl.pallas_call(
        functools.partial(_da_body, lam_init=lam_init),
        grid=(bsz, DA_HEADS, seq // tq),
        in_specs=[pl.BlockSpec((1, tq, dv), lambda b, h, i: (b, i, h)),
                  pl.BlockSpec((1, seq, dv), lambda b, h, i: (b, 0, DA_HEADS + h)),
                  pl.BlockSpec((1, seq, dv), lambda b, h, i: (b, 0, 2 * DA_HEADS + h)),
                  pl.BlockSpec((4, DA_DH), lambda b, h, i: (0, 0)),
                  pl.BlockSpec((1, dv), lambda b, h, i: (0, 0))],
        out_specs=pl.BlockSpec((1, tq, dv), lambda b, h, i: (b, i, h)),
        out_shape=jax.ShapeDtypeStruct((bsz, seq, DA_HEADS * dv), BF16),
        compiler_params=_cparams(("arbitrary", "arbitrary", "arbitrary")),
        name="diff_attn",
    )(proj, proj, proj, diff_lambda, diff_norm.reshape(1, dv))


def _route_body(x_ref, g_ref, mod_ref, rwc_ref, rwh_ref, rb_ref,
                h_ref, e_ref, gate_ref, rank_ref, cnt_ref, run_s, *, tm):
    i = pl.program_id(0)

    @pl.when(i == 0)
    def _():
        run_s[...] = jnp.zeros_like(run_s)

    h = _norm_mod(x_ref[...], g_ref[...], mod_ref[0, 3:4, :], mod_ref[0, 4:5, :])
    h_ref[...] = h
    h_hi = h.astype(BF16)
    h_lo = (h - h_hi.astype(F32)).astype(BF16)
    both = _dot_nt(rwc_ref[...], h_hi)
    logits = both[0:N_EXPERTS] + both[N_EXPERTS:2 * N_EXPERTS] + _dot_nt(rwh_ref[...], h_lo)
    scores = jax.nn.sigmoid(logits)
    sel = scores + rb_ref[...]

    sc = [scores[e:e + 1, :] for e in range(N_EXPERTS)]
    sl = [sel[e:e + 1, :] for e in range(N_EXPERTS)]

    def top2_sum(a, b, c, d):
        hi1, lo1 = jnp.maximum(a, b), jnp.minimum(a, b)
        hi2, lo2 = jnp.maximum(c, d), jnp.minimum(c, d)
        return jnp.maximum(hi1, hi2) + jnp.maximum(jnp.minimum(hi1, hi2), jnp.maximum(lo1, lo2))

    gs = [top2_sum(*sl[4 * g:4 * g + 4]) for g in range(N_GROUPS)]
    best = jnp.zeros_like(gs[0], dtype=I32)
    best_v = gs[0]
    for g in range(1, N_GROUPS):
        better = gs[g] > best_v
        best = jnp.where(better, g, best)
        best_v = jnp.where(better, gs[g], best_v)

    def pick(vals, j):
        out = vals[j]
        for g in range(1, N_GROUPS):
            out = jnp.where(best == g, vals[4 * g + j], out)
        return out

    v = [pick(sl, j) for j in range(EXPERTS_PER_GROUP)]
    w = [pick(sc, j) for j in range(EXPERTS_PER_GROUP)]

    def argmax4(vals, excluded):
        idx = jnp.full_like(best, -1)
        cur = jnp.full_like(vals[0], -jnp.inf)
        for j in range(EXPERTS_PER_GROUP):
            ok = (vals[j] > cur) if excluded is None else ((vals[j] > cur) & (excluded != j))
            idx = jnp.where(ok, j, idx)
            cur = jnp.where(ok, vals[j], cur)
        return idx

    i1 = argmax4(v, None)
    i2 = argmax4(v, i1)

    def take(vals, idx):
        out = vals[0]
        for j in range(1, EXPERTS_PER_GROUP):
            out = jnp.where(idx == j, vals[j], out)
        return out

    w1, w2 = take(w, i1), take(w, i2)
    den = w1 + w2
    e1 = best * EXPERTS_PER_GROUP + i1
    e2 = best * EXPERTS_PER_GROUP + i2
    e_ref[0:1, :] = e1
    e_ref[1:2, :] = e2
    gate_ref[0:1, :] = w1 / den
    gate_ref[1:2, :] = w2 / den

    eid = lax.broadcasted_iota(I32, (N_EXPERTS, tm), 0)
    oh1 = eid == e1
    oh2 = eid == e2
    onehot = (oh1 | oh2).astype(BF16)
    srow = lax.broadcasted_iota(I32, (tm, tm), 0)
    tcol = lax.broadcasted_iota(I32, (tm, tm), 1)
    before = (srow < tcol).astype(BF16)
    pos = _dot(onehot, before) + run_s[...]
    rank_ref[0:1, :] = jnp.sum(jnp.where(oh1, pos, 0.0), axis=0, keepdims=True).astype(I32)
    rank_ref[1:2, :] = jnp.sum(jnp.where(oh2, pos, 0.0), axis=0, keepdims=True).astype(I32)
    run_s[...] = run_s[...] + jnp.sum(onehot.astype(F32), axis=1, keepdims=True)
    cnt_ref[...] = jnp.broadcast_to(run_s[...], cnt_ref.shape).astype(I32)


def _route(x, g, mod, router_w, router_b, seq, tm=512):
    n, d = x.shape
    tpb = seq // tm
    rwt = router_w.T
    rw_hi = rwt.astype(BF16)
    rw_lo = (rwt - rw_hi.astype(F32)).astype(BF16)
    rwc = jnp.concatenate([rw_hi, rw_lo], axis=0)
    pair_spec = pl.BlockSpec((2, tm), lambda i: (0, i))
    return pl.pallas_call(
        functools.partial(_route_body, tm=tm),
        grid=(n // tm,),
        in_specs=[pl.BlockSpec((tm, d), lambda i: (i, 0)),
                  pl.BlockSpec((1, d), lambda i: (0, 0)),
                  pl.BlockSpec((1, 6, d), lambda i: (i // tpb, 0, 0)),
                  pl.BlockSpec((2 * N_EXPERTS, d), lambda i: (0, 0)),
                  pl.BlockSpec((N_EXPERTS, d), lambda i: (0, 0)),
                  pl.BlockSpec((N_EXPERTS, 1), lambda i: (0, 0))],
        out_specs=[pl.BlockSpec((tm, d), lambda i: (i, 0)), pair_spec, pair_spec, pair_spec,
                   pl.BlockSpec((N_EXPERTS, LANES), lambda i: (0, 0))],
        out_shape=[jax.ShapeDtypeStruct((n, d), F32),
                   jax.ShapeDtypeStruct((2, n), I32),
                   jax.ShapeDtypeStruct((2, n), F32),
                   jax.ShapeDtypeStruct((2, n), I32),
                   jax.ShapeDtypeStruct((N_EXPERTS, LANES), I32)],
        scratch_shapes=[pltpu.VMEM((N_EXPERTS, 1), F32)],
        compiler_params=_cparams(("arbitrary",)),
        name="moe_route",
    )(x, g.reshape(1, d), mod, rwc, rw_hi, router_b.reshape(N_EXPERTS, 1).astype(F32))


def _row_copy(src_ref, s, dst_ref, d, sem):
    return pltpu.make_async_copy(src_ref.at[pl.ds(s, 1), :], dst_ref.at[pl.ds(d, 1), :], sem)


def _dispatch_body(meta_ref, dest_ref, h_ref, xs_ref, zero_s, sem, *, tmd, tm, max_tiles):
    i = pl.program_id(0)
    zr = zero_s.shape[0]

    def zero_tile(start):
        copies = [pltpu.make_async_copy(zero_s, xs_ref.at[pl.ds(pl.multiple_of(start + z * zr, zr), zr), :], sem)
                  for z in range(tm // zr)]
        for cp in copies:
            cp.start()
        for cp in copies:
            cp.wait()

    @pl.when(i == 0)
    def _():
        zero_s[...] = jnp.zeros_like(zero_s)
        for e in range(N_EXPERTS):
            @pl.when(meta_ref[N_EXPERTS + e] > 0)
            def _():
                zero_tile(pl.multiple_of(meta_ref[e] - tm, tm))

        def fill_unused(t, carry):
            zero_tile(pl.multiple_of(t * tm, tm))
            return carry

        lax.fori_loop(meta_ref[2 * N_EXPERTS], max_tiles, fill_unused, 0)

    def issue(t, carry):
        _row_copy(h_ref, t, xs_ref, dest_ref[0, 0, t], sem).start(priority=0)
        _row_copy(h_ref, t, xs_ref, dest_ref[0, 0, tmd + t], sem).start(priority=1)
        return carry

    lax.fori_loop(0, tmd, issue, 0, unroll=4)

    def drain(t, carry):
        _row_copy(h_ref, 0, xs_ref, 0, sem).wait()
        _row_copy(h_ref, 0, xs_ref, 0, sem).wait()
        return carry

    lax.fori_loop(0, tmd, drain, 0)


def _dispatch(h, dest, meta, n_slots, tm, tmd=256):
    n, d = h.shape
    dest_blk = jnp.concatenate([dest[0].reshape(n // tmd, 1, tmd), dest[1].reshape(n // tmd, 1, tmd)], axis=2)
    grid_spec = pltpu.PrefetchScalarGridSpec(
        num_scalar_prefetch=1,
        grid=(n // tmd,),
        in_specs=[pl.BlockSpec((1, 1, 2 * tmd), lambda i, m: (i, 0, 0), memory_space=pltpu.SMEM),
                  pl.BlockSpec((tmd, d), lambda i, m: (i, 0))],
        out_specs=pl.BlockSpec(memory_space=pl.ANY),
        scratch_shapes=[pltpu.VMEM((ZERO_ROWS, d), F32), pltpu.SemaphoreType.DMA(())],
    )
    return pl.pallas_call(
        functools.partial(_dispatch_body, tmd=tmd, tm=tm, max_tiles=n_slots // tm),
        grid_spec=grid_spec,
        out_shape=jax.ShapeDtypeStruct((n_slots, d), F32),
        compiler_params=_cparams(("arbitrary",)),
        name="moe_dispatch",
    )(meta, dest_blk, h)


def _moe_up_body(te_ref, nv_ref, xs_ref, wg_ref, wu_ref, hid_ref, wgb_s, wub_s):
    i = pl.program_id(1)
    valid = i < nv_ref[0]
    changed = (i == 0) | (te_ref[i] != te_ref[jnp.maximum(i - 1, 0)])

    @pl.when(valid & changed)
    def _():
        wgb_s[...] = wg_ref[0].astype(BF16)
        wub_s[...] = wu_ref[0].astype(BF16)

    @pl.when(valid)
    def _():
        x = xs_ref[...].astype(BF16)
        a = _dot(x, wgb_s[...])
        u = _dot(x, wub_s[...])
        hid_ref[...] = (_silu(a) * u).astype(hid_ref.dtype)

    @pl.when(jnp.logical_not(valid))
    def _():
        hid_ref[...] = jnp.zeros_like(hid_ref)


def _moe_up(xs, wg, wu, te, nv, tm, fc=512):
    n_slots, d = xs.shape
    ff = wg.shape[2]
    max_tiles = n_slots // tm

    def row_map(j, i, te_r, nv_r):
        return (jnp.minimum(i, nv_r[0] - 1), 0)

    def w_map(j, i, te_r, nv_r):
        return (te_r[i], 0, j)

    grid_spec = pltpu.PrefetchScalarGridSpec(
        num_scalar_prefetch=2,
        grid=(ff // fc, max_tiles),
        in_specs=[pl.BlockSpec((tm, d), row_map),
                  pl.BlockSpec((1, d, fc), w_map),
                  pl.BlockSpec((1, d, fc), w_map)],
        out_specs=pl.BlockSpec((tm, fc), lambda j, i, te_r, nv_r: (i, j)),
        scratch_shapes=[pltpu.VMEM((d, fc), BF16), pltpu.VMEM((d, fc), BF16)],
    )
    return pl.pallas_call(
        _moe_up_body,
        grid_spec=grid_spec,
        out_shape=jax.ShapeDtypeStruct((n_slots, ff), BF16),
        compiler_params=_cparams(("arbitrary", "arbitrary")),
        name="moe_up",
    )(te, nv, xs, wg, wu)


def _moe_down_body(te_ref, nv_ref, hid_ref, wd_ref, o_ref, wdb_s):
    i = pl.program_id(1)
    valid = i < nv_ref[0]
    changed = (i == 0) | (te_ref[i] != te_ref[jnp.maximum(i - 1, 0)])

    @pl.when(valid & changed)
    def _():
        wdb_s[...] = wd_ref[0].astype(BF16)

    @pl.when(valid)
    def _():
        o_ref[...] = _dot(hid_ref[...], wdb_s[...])

    @pl.when(jnp.logical_not(valid))
    def _():
        o_ref[...] = jnp.zeros_like(o_ref)


def _moe_down(hid, wd, te, nv, tm, nc=2048):
    n_slots, ff = hid.shape
    d = wd.shape[2]
    max_tiles = n_slots // tm
    grid_spec = pltpu.PrefetchScalarGridSpec(
        num_scalar_prefetch=2,
        grid=(d // nc, max_tiles),
        in_specs=[pl.BlockSpec((tm, ff), lambda j, i, te_r, nv_r: (jnp.minimum(i, nv_r[0] - 1), 0)),
                  pl.BlockSpec((1, ff, nc), lambda j, i, te_r, nv_r: (te_r[i], 0, j))],
        out_specs=pl.BlockSpec((tm, nc), lambda j, i, te_r, nv_r: (i, j)),
        scratch_shapes=[pltpu.VMEM((ff, nc), BF16)],
    )
    return pl.pallas_call(
        _moe_down_body,
        grid_spec=grid_spec,
        out_shape=jax.ShapeDtypeStruct((n_slots, d), F32),
        compiler_params=_cparams(("arbitrary", "arbitrary")),
        name="moe_down",
    )(te, nv, hid, wd)


def _combine_body(dest_ref, x_ref, gate_ref, mod_ref, ys_ref, *rest, tc, final, with_next):
    hn_ref = None
    if final:
        lnf_ref, o_ref, buf_s, sem = rest
    elif with_next:
        lng_ref, modn_ref, o_ref, hn_ref, buf_s, sem = rest
    else:
        o_ref, buf_s, sem = rest

    i = pl.program_id(0)
    n_tiles = pl.num_programs(0) - 1
    slot = i % 2

    def row_in(s, k, t, d):
        return pltpu.make_async_copy(ys_ref.at[pl.ds(d, 1), :], buf_s.at[s, k, pl.ds(t, 1), :], sem.at[s])

    @pl.when(i < n_tiles)
    def _():
        def issue(t, carry):
            row_in(slot, 0, t, dest_ref[0, 0, t]).start(priority=0)
            row_in(slot, 1, t, dest_ref[0, 0, tc + t]).start(priority=1)
            return carry

        lax.fori_loop(0, tc, issue, 0, unroll=4)

    @pl.when(i > 0)
    def _():
        prev = 1 - slot

        def drain(t, carry):
            row_in(prev, 0, 0, 0).wait()
            row_in(prev, 1, 0, 0).wait()
            return carry

        lax.fori_loop(0, tc, drain, 0)

        gates = gate_ref[...]
        y = x_ref[...] + mod_ref[0, 5:6, :] * (gates[:, 0:1] * buf_s[prev, 0] + gates[:, 1:2] * buf_s[prev, 1])
        if final:
            y = _rms(y) * lnf_ref[...]
        o_ref[...] = y
        if hn_ref is not None:
            hn_ref[...] = _norm_mod(y, lng_ref[...], modn_ref[0, 0:1, :], modn_ref[0, 1:2, :]).astype(hn_ref.dtype)


def _combine(x, ys, dest, gates, mod, seq, ln_final=None, next_norm=None, tc=256):
    n, d = x.shape
    tpb = seq // tc
    final = ln_final is not None
    with_next = next_norm is not None
    n_tiles = n // tc
    dest_blk = jnp.concatenate([dest[0].reshape(n_tiles, 1, tc), dest[1].reshape(n_tiles, 1, tc)], axis=2)

    def done(i):
        return jnp.maximum(i - 1, 0)

    row_spec = pl.BlockSpec((tc, d), lambda i: (done(i), 0))
    mod_spec = pl.BlockSpec((1, 6, d), lambda i: (done(i) // tpb, 0, 0))
    vec_spec = pl.BlockSpec((1, d), lambda i: (0, 0))
    in_specs = [pl.BlockSpec((1, 1, 2 * tc), lambda i: (jnp.minimum(i, n_tiles - 1), 0, 0), memory_space=pltpu.SMEM),
                row_spec,
                pl.BlockSpec((tc, 2), lambda i: (done(i), 0)),
                mod_spec,
                pl.BlockSpec(memory_space=pl.ANY)]
    args = [dest_blk, x, gates.T, mod, ys]
    if final:
        in_specs.append(vec_spec)
        args.append(ln_final.reshape(1, d))
    out_specs, out_shape = row_spec, jax.ShapeDtypeStruct((n, d), F32)
    if with_next:
        in_specs += [vec_spec, mod_spec]
        args += [next_norm[0].reshape(1, d), next_norm[1]]
        out_specs, out_shape = [row_spec, row_spec], [out_shape, jax.ShapeDtypeStruct((n, d), BF16)]
    return pl.pallas_call(
        functools.partial(_combine_body, tc=tc, final=final, with_next=with_next),
        grid=(n_tiles + 1,),
        in_specs=in_specs,
        out_specs=out_specs,
        out_shape=out_shape,
        scratch_shapes=[pltpu.VMEM((2, 2, tc, d), F32), pltpu.SemaphoreType.DMA((2,))],
        compiler_params=_cparams(("arbitrary",)),
        name="moe_combine",
    )(*args)


def _moe(x, ln_g, mod, router_w, router_b, wg, wu, wd, seq, ln_final=None, next_norm=None, tm=512):
    n, d = x.shape
    h, eidx, gates, rank, cnt = _route(x, ln_g, mod, router_w, router_b, seq)
    counts = cnt[:, 0]
    ntile = (counts + tm - 1) // tm
    cum = jnp.cumsum(ntile)
    off = (cum - ntile) * tm
    max_tiles = (2 * n) // tm + N_EXPERTS
    nv = cum[-1:].astype(I32)
    tiles = jnp.minimum(jnp.arange(max_tiles, dtype=I32), nv[0] - 1)
    te = jnp.minimum(jnp.sum((tiles[:, None] >= cum[None, :]).astype(I32), axis=1), N_EXPERTS - 1)
    expert_ids = jnp.arange(N_EXPERTS, dtype=I32)[:, None, None]
    dest = rank + jnp.sum(jnp.where(eidx[None] == expert_ids, off[:, None, None], 0), axis=0)
    dest = dest.astype(I32)
    meta = jnp.concatenate([cum * tm, ntile, nv]).astype(I32)
    xs = _dispatch(h, dest, meta, max_tiles * tm, tm)
    hid = _moe_up(xs, wg, wu, te, nv, tm)
    ys = _moe_down(hid, wd, te, nv, tm)
    return _combine(x, ys, dest, gates, mod, seq, ln_final, next_norm)


def kernel(x, c, ln_attn_0, ada_w_0, ada_b_0, w_in_0, hgrn_norm_0, na_rpb_0, w_out_0, ln_ffn_0, moe_wg_0, moe_wu_0, moe_wd_0, ln_attn_1, ada_w_1, ada_b_1, w_in_1, diff_lambda_1, diff_norm_1, w_out_1, ln_ffn_1, moe_wg_1, moe_wu_1, moe_wd_1, hgrn_lb_logits, router_w, router_b, ln_final):
    bsz, seq, d = x.shape
    n = bsz * seq
    xf = x.reshape(n, d)
    mod0 = _ada(c, ada_w_0, ada_b_0)
    mod1 = _ada(c, ada_w_1, ada_b_1)

    h = _norm(xf, ln_attn_0, mod0, 0, 1, seq)
    proj = _mm(h, w_in_0.astype(BF16)).reshape(bsz, seq, -1)
    o_a = _hgrn(proj, hgrn_lb_logits, hgrn_norm_0)
    o_b = _na(proj, na_rpb_0, 5 * HG_HEADS * HG_D)
    xf = _mm_res([o_a.reshape(n, -1), o_b.reshape(n, -1)], w_out_0.astype(BF16), xf, mod0, 2, seq)
    xf, h = _moe(xf, ln_ffn_0, mod0, router_w, router_b, moe_wg_0, moe_wu_0, moe_wd_0, seq,
                 next_norm=(ln_attn_1, mod1))

    proj = _mm_rope(h, w_in_1.astype(BF16), seq, 4 * DA_HEADS * DA_DH).reshape(bsz, seq, -1)
    o_c = _da(proj, diff_lambda_1, diff_norm_1, 1)
    xf = _mm_res([o_c.reshape(n, -1)], w_out_1.astype(BF16), xf, mod1, 2, seq)
    xf = _moe(xf, ln_ffn_1, mod1, router_w, router_b, moe_wg_1, moe_wu_1, moe_wd_1, seq, ln_final=ln_final)
    return xf.reshape(bsz, seq, d)
```
